```python
import jax, jax.numpy as jnp
from jax import lax
import numpy as np

D_MODEL = 1024
BATCH = 4
SEQ = 8192
DEPTH = 1
DEC_BATCH = 128
DEC_SEQ = 1
PAST_LEN = 16384
PAGE_SIZE = 128

N_HEADS = 4
QK_NOPE = 128
QK_ROPE = 64
V_DIM = 128
Q_RANK = 384
KV_RANK = 256
ATTN_WIDTH = N_HEADS * V_DIM
CONV_WIDTH = D_MODEL - ATTN_WIDTH
CONV_GROUPS = 4
CONV_K = 3
D_MIX = ATTN_WIDTH + CONV_WIDTH
D_IN = Q_RANK + KV_RANK + QK_ROPE + 3 * CONV_WIDTH
PEER_HEADS = 8
PEER_KEYS = 128
PEER_EXPERTS = PEER_KEYS * PEER_KEYS
PEER_DK = 256
PEER_TOPK = 16
PEER_CHUNK = 128
Q_BLOCK = 128
ROPE_THETA = 10000.0
EPS = 1e-6
SM_SCALE = (QK_NOPE + QK_ROPE) ** -0.5

kernel_name = "hymba_mla_shortconv_peer_step"


def rmsnorm(x, g):
    xf = x.astype(jnp.float32)
    y = xf * lax.rsqrt(jnp.mean(xf * xf, axis=-1, keepdims=True) + EPS)
    return (y * g.astype(jnp.float32)).astype(x.dtype)


def group_rmsnorm(x, g, groups):
    shp = x.shape
    xg = x.reshape(shp[:-1] + (groups, shp[-1] // groups))
    return rmsnorm(xg, g.reshape(groups, shp[-1] // groups)).reshape(shp)


def rope(x, pos):
    p = x.shape[-1]
    inv = ROPE_THETA ** (-jnp.arange(0, p, 2, dtype=jnp.float32) / p)
    ang = pos.astype(jnp.float32)[:, None] * inv[None, :]
    cos = jnp.cos(ang)[None, :, None, :]
    sin = jnp.sin(ang)[None, :, None, :]
    xf = x.astype(jnp.float32)
    x1, x2 = xf[..., : p // 2], xf[..., p // 2:]
    return jnp.concatenate([x1 * cos - x2 * sin, x2 * cos + x1 * sin], axis=-1).astype(x.dtype)


def project(xn, pos, w_in, g_q, w_uq, w_uk, g_kv):
    h = xn @ w_in
    o1 = Q_RANK
    o2 = o1 + KV_RANK
    o3 = o2 + QK_ROPE
    o4 = o3 + CONV_WIDTH
    o5 = o4 + CONV_WIDTH
    cq, ckv, kr, z_in, z_c, z_b = jnp.split(h, [o1, o2, o3, o4, o5], axis=-1)
    q = jnp.einsum('bsr,rhd->bshd', rmsnorm(cq, g_q), w_uq)
    q_nope, q_rope = q[..., :QK_NOPE], q[..., QK_NOPE:]
    q_rope = rope(q_rope, pos)
    q_lat = jnp.einsum('bshd,rhd->bshr', q_nope, w_uk)
    ckv = rmsnorm(ckv, g_kv)
    kr = rope(kr[:, :, None, :], pos)[:, :, 0, :]
    z = z_c * z_in
    return q_lat, q_rope, ckv, kr, z, z_b


def latent_attention(q_lat, q_rope, ckv, kr, mask):
    s = (jnp.einsum('bqhr,bkr->bhqk', q_lat, ckv)
         + jnp.einsum('bqhp,bkp->bhqk', q_rope, kr)).astype(jnp.float32) * SM_SCALE
    s = jnp.where(mask, s, -jnp.inf)
    p = jax.nn.softmax(s, axis=-1).astype(ckv.dtype)
    return jnp.einsum('bhqk,bkr->bqhr', p, ckv)


def prompt_attention(q_lat, q_rope, ckv, kr):
    b, s, h, r = q_lat.shape
    nb = s // Q_BLOCK
    ql = q_lat.reshape(b, nb, Q_BLOCK, h, r).swapaxes(0, 1)
    qr = q_rope.reshape(b, nb, Q_BLOCK, h, QK_ROPE).swapaxes(0, 1)
    kpos = jnp.arange(s)

    def blk(args):
        qlb, qrb, i = args
        qpos = i * Q_BLOCK + jnp.arange(Q_BLOCK)
        mask = kpos[None, :] <= qpos[:, None]
        return latent_attention(qlb, qrb, ckv, kr, mask)

    o = lax.map(blk, (ql, qr, jnp.arange(nb)))
    return o.swapaxes(0, 1).reshape(b, s, h, r)


def short_conv(z_ext, w):
    s = z_ext.shape[1] - (CONV_K - 1)
    y = w[0] * z_ext[:, 0:s]
    for k in range(1, CONV_K):
        y = y + w[k] * z_ext[:, k:k + s]
    return y


def mix_out(o_lat, y_conv, z_b, w_uv, g_attn_out, g_conv_out, w_o):
    b, s = o_lat.shape[:2]
    attn = jnp.einsum('bshr,rhv->bshv', o_lat, w_uv).reshape(b, s, ATTN_WIDTH)
    attn = group_rmsnorm(attn, g_attn_out, N_HEADS)
    conv = group_rmsnorm(z_b * y_conv, g_conv_out, CONV_GROUPS)
    return jnp.concatenate([attn, conv], axis=-1) @ w_o


def peer_block(x, w_query, sub_keys, expert_u, expert_v):
    t = x.shape[0]
    q = (x @ w_query).reshape(t, PEER_HEADS, 2, PEER_DK // 2)
    s = jnp.einsum('thcd,ckd->thck', q, sub_keys).astype(jnp.float32)
    sv, si = lax.top_k(s, PEER_TOPK)
    cand = (sv[:, :, 0, :, None] + sv[:, :, 1, None, :]).reshape(t, PEER_HEADS, PEER_TOPK * PEER_TOPK)
    cidx = (si[:, :, 0, :, None] * PEER_KEYS + si[:, :, 1, None, :]).reshape(t, PEER_HEADS, PEER_TOPK * PEER_TOPK)
    top_s, pos = lax.top_k(cand, PEER_TOPK)
    experts = jnp.take_along_axis(cidx, pos, axis=-1)
    g = jax.nn.softmax(top_s, axis=-1)
    u = expert_u[experts]
    v = expert_v[experts]
    a = jax.nn.gelu(jnp.einsum('td,thkd->thk', x, u).astype(jnp.float32), approximate=False)
    return jnp.einsum('thk,thkd->td', (g * a).astype(x.dtype), v)


def peer_prompt(xn, w_query, sub_keys, expert_u, expert_v):
    b, s, d = xn.shape
    xc = xn.reshape(-1, PEER_CHUNK, d)
    out = lax.map(lambda c: peer_block(c, w_query, sub_keys, expert_u, expert_v), xc)
    return out.reshape(b, s, d)


def setup_inputs(seed: int = 0) -> dict:
    key = jax.random.key(seed)
    ks = jax.random.split(key, 32)
    n_pages = PAST_LEN // PAGE_SIZE
    in_use = DEC_BATCH * n_pages
    n_pool = in_use + max(1, in_use // 4)
    f32 = jnp.float32

    def nrm(k, shape, scale):
        return jax.random.normal(k, shape, f32) * scale

    def gain(k, shape):
        return 1.0 + 0.02 * jax.random.normal(k, shape, f32)

    page_table = jax.random.permutation(ks[5], n_pool)[:in_use].reshape(DEC_BATCH, n_pages).astype(jnp.int32)
    return {
        "x_prompt": nrm(ks[0], (BATCH, SEQ, D_MODEL), 1.0),
        "x_sample": nrm(ks[1], (DEC_BATCH, DEC_SEQ, D_MODEL), 1.0),
        "cache_ckv": nrm(ks[2], (DEPTH, n_pool, PAGE_SIZE, KV_RANK), 1.0),
        "cache_krope": nrm(ks[3], (DEPTH, n_pool, PAGE_SIZE, QK_ROPE), 1.0),
        "state_conv": nrm(ks[4], (DEPTH, DEC_BATCH, CONV_K - 1, CONV_WIDTH), 1.0),
        "page_table": page_table,
        "g_mix_norm": gain(ks[6], (DEPTH, D_MODEL)),
        "w_in": nrm(ks[7], (DEPTH, D_MODEL, D_IN), D_MODEL ** -0.5),
        "g_q": gain(ks[8], (DEPTH, Q_RANK)),
        "w_uq": nrm(ks[9], (DEPTH, Q_RANK, N_HEADS, QK_NOPE + QK_ROPE), Q_RANK ** -0.5),
        "g_kv": gain(ks[10], (DEPTH, KV_RANK)),
        "w_uk": nrm(ks[11], (DEPTH, KV_RANK, N_HEADS, QK_NOPE), KV_RANK ** -0.5),
        "w_uv": nrm(ks[12], (DEPTH, KV_RANK, N_HEADS, V_DIM), KV_RANK ** -0.5),
        "conv_w": nrm(ks[13], (DEPTH, CONV_K, CONV_WIDTH), CONV_K ** -0.5),
        "g_attn_out": gain(ks[14], (DEPTH, ATTN_WIDTH)),
        "g_conv_out": gain(ks[15], (DEPTH, CONV_WIDTH)),
        "w_o": nrm(ks[16], (DEPTH, D_MIX, D_MODEL), D_MIX ** -0.5),
        "g_ffn_norm": gain(ks[17], (DEPTH, D_MODEL)),
        "w_query": nrm(ks[18], (DEPTH, D_MODEL, PEER_HEADS * PEER_DK), D_MODEL ** -0.5),
        "sub_keys": nrm(ks[19], (DEPTH, 2, PEER_KEYS, PEER_DK // 2), (PEER_DK // 2) ** -0.5),
        "expert_u": nrm(ks[20], (DEPTH, PEER_EXPERTS, D_MODEL), D_MODEL ** -0.5),
        "expert_v": nrm(ks[21], (DEPTH, PEER_EXPERTS, D_MODEL), 0.5),
        "g_final": gain(ks[22], (D_MODEL,)),
    }


def reference(x_prompt, x_sample, cache_ckv, cache_krope, state_conv, page_table,
              g_mix_norm, w_in, g_q, w_uq, g_kv, w_uk, w_uv, conv_w, g_attn_out, g_conv_out,
              w_o, g_ffn_norm, w_query, sub_keys, expert_u, expert_v, g_final):
    dec_b, n_pages = page_table.shape
    past = n_pages * PAGE_SIZE
    s_p = x_prompt.shape[1]
    s_d = x_sample.shape[1]
    pos_p = jnp.arange(s_p)
    pos_s = past + jnp.arange(s_d)
    mask_s = jnp.arange(past + s_d)[None, :] <= pos_s[:, None]

    xp, xs = x_prompt, x_sample
    ckv_p_l, kr_p_l, conv_p_l, ckv_s_l, kr_s_l, conv_s_l = [], [], [], [], [], []
    for l in range(DEPTH):
        eu, ev = expert_u[l], expert_v[l]
        xn = rmsnorm(xp, g_mix_norm[l])
        q_lat, q_rope, ckv, kr, z, z_b = project(xn, pos_p, w_in[l], g_q[l], w_uq[l], w_uk[l], g_kv[l])
        o_lat = prompt_attention(q_lat, q_rope, ckv, kr)
        z_ext = jnp.concatenate([jnp.zeros((z.shape[0], CONV_K - 1, CONV_WIDTH), z.dtype), z], axis=1)
        y_conv = short_conv(z_ext, conv_w[l])
        xp = xp + mix_out(o_lat, y_conv, z_b, w_uv[l], g_attn_out[l], g_conv_out[l], w_o[l])
        xp = xp + peer_prompt(rmsnorm(xp, g_ffn_norm[l]), w_query[l], sub_keys[l], eu, ev)
        ckv_p_l.append(ckv)
        kr_p_l.append(kr)
        conv_p_l.append(z_ext[:, -(CONV_K - 1):])
        xn = rmsnorm(xs, g_mix_norm[l])
        q_lat, q_rope, ckv, kr, z, z_b = project(xn, pos_s, w_in[l], g_q[l], w_uq[l], w_uk[l], g_kv[l])
        past_ckv = cache_ckv[l, page_table].reshape(dec_b, past, KV_RANK)
        past_kr = cache_krope[l, page_table].reshape(dec_b, past, QK_ROPE)
        keys_ckv = jnp.concatenate([past_ckv, ckv], axis=1)
        keys_kr = jnp.concatenate([past_kr, kr], axis=1)
        o_lat = latent_attention(q_lat, q_rope, keys_ckv, keys_kr, mask_s)
        z_ext = jnp.concatenate([state_conv[l], z], axis=1)
        y_conv = short_conv(z_ext, conv_w[l])
        xs = xs + mix_out(o_lat, y_conv, z_b, w_uv[l], g_attn_out[l], g_conv_out[l], w_o[l])
        xs2 = rmsnorm(xs, g_ffn_norm[l])
        xs = xs + peer_block(xs2.reshape(-1, D_MODEL), w_query[l], sub_keys[l], eu, ev).reshape(xs.shape)
        ckv_s_l.append(ckv)
        kr_s_l.append(kr)
        conv_s_l.append(z_ext[:, -(CONV_K - 1):])

    y_prompt = rmsnorm(xp, g_final)
    y_sample = rmsnorm(xs, g_final)
    return (y_prompt, y_sample,
            jnp.stack(ckv_p_l), jnp.stack(kr_p_l), jnp.stack(conv_p_l),
            jnp.stack(ckv_s_l), jnp.stack(kr_s_l), jnp.stack(conv_s_l))
```

```python
import functools

import jax
import jax.numpy as jnp
import numpy as np
from jax import lax
from jax.experimental import pallas as pl
from jax.experimental.pallas import tpu as pltpu

D_MODEL = 1024
N_HEADS = 4
QK_NOPE = 128
QK_ROPE = 64
V_DIM = 128
Q_RANK = 384
KV_RANK = 256
ATTN_WIDTH = N_HEADS * V_DIM
CONV_WIDTH = D_MODEL - ATTN_WIDTH
CONV_GROUPS = 4
CONV_K = 3
PEER_HEADS = 8
PEER_KEYS = 128
PEER_DK = 256
PEER_TOPK = 16
PAGE_SIZE = 128
ROPE_THETA = 10000.0
EPS = 1e-6
SM_SCALE = (QK_NOPE + QK_ROPE) ** -0.5

LANES = 128
SUBLANES = 8
VMEM_LIMIT = 48 * 1024 * 1024

QK_PAD = 384
N_SEL = PEER_HEADS * PEER_TOPK
ROWS_PER_EXPERT = D_MODEL // LANES
GROUPS = 2 * PEER_HEADS
D_INX = Q_RANK + KV_RANK + 2 * QK_ROPE + 3 * CONV_WIDTH
NEG_INF = float("-inf")

_NT = (((1,), (1,)), ((), ()))


def _cparams(*sem):
    return pltpu.CompilerParams(dimension_semantics=sem, vmem_limit_bytes=VMEM_LIMIT)


def _rms(x, g):
    return x * lax.rsqrt(jnp.mean(x * x, axis=-1, keepdims=True) + EPS) * g


def _bf(x):
    return x.astype(jnp.bfloat16)


def _rope_pair(t, cs):
    r = t * cs
    return r + pltpu.roll(r, QK_ROPE, axis=1)


def _proj_kernel(seq_conv, tm, x_ref, gmix_ref, win_ref, gq_ref, wq_ref, wuk_ref, gkv_ref,
                 convw_ref, cs_ref, s0_ref, s1_ref, gconv_ref,
                 qcat_ref, kcat_ref, ckv_ref, kr_ref, convn_ref, z_ref, zbuf_ref):
    si = pl.program_id(1)
    x = x_ref[0]
    xn = _bf(_rms(x, gmix_ref[...]))
    h = jnp.dot(xn, win_ref[...], preferred_element_type=jnp.float32)
    o1 = Q_RANK
    o2 = o1 + KV_RANK
    o3 = o2 + 2 * QK_ROPE
    o4 = o3 + CONV_WIDTH
    o5 = o4 + CONV_WIDTH
    cs = cs_ref[...]
    lane = lax.broadcasted_iota(jnp.int32, (tm, LANES), 1)
    rope_mask = lane < QK_ROPE

    ckv = _rms(h[:, o1:o2], gkv_ref[...])
    kr = _rope_pair(h[:, o2:o3], cs)
    ckv_ref[0] = ckv
    kr_ref[0] = kr[:, :QK_ROPE]
    kcat_ref[0, :, 0:KV_RANK] = _bf(ckv)
    kcat_ref[0, :, KV_RANK:QK_PAD] = _bf(jnp.where(rope_mask, kr, 0.0))

    cq = _bf(_rms(h[:, 0:o1], gq_ref[...]))
    q = jnp.dot(cq, wq_ref[...], preferred_element_type=jnp.float32)
    for hd in range(N_HEADS):
        qn = _bf(q[:, hd * QK_NOPE:(hd + 1) * QK_NOPE])
        qlat = jnp.dot(qn, wuk_ref[hd], preferred_element_type=jnp.float32)
        base = N_HEADS * QK_NOPE + hd * LANES
        qr = _rope_pair(q[:, base:base + LANES], cs)
        qcat_ref[0, hd, :, 0:KV_RANK] = _bf(qlat * SM_SCALE)
        qcat_ref[0, hd, :, KV_RANK:QK_PAD] = _bf(jnp.where(rope_mask, qr * SM_SCALE, 0.0))

    z = h[:, o4:o5] * h[:, o3:o4]
    zb = h[:, o5:]
    w0 = convw_ref[0:1, :]
    w1 = convw_ref[1:2, :]
    w2 = convw_ref[2:3, :]
    if seq_conv:
        @pl.when(si == 0)
        def _():
            zbuf_ref[0:SUBLANES, :] = jnp.zeros((SUBLANES, CONV_WIDTH), jnp.float32)

        zbuf_ref[SUBLANES:SUBLANES + tm, :] = z
        y = (w2 * z + w1 * zbuf_ref[SUBLANES - 1:SUBLANES - 1 + tm, :]
             + w0 * zbuf_ref[SUBLANES - 2:SUBLANES - 2 + tm, :])
        tail = zbuf_ref[tm:tm + SUBLANES, :]
        zbuf_ref[0:SUBLANES, :] = tail
        z_ref[0] = tail
    else:
        y = w2 * z + w1 * s1_ref[...] + w0 * s0_ref[...]
        z_ref[0] = z
    c = zb * y
    for g in range(CONV_GROUPS):
        sl = slice(g * LANES, (g + 1) * LANES)
        convn_ref[0, :, sl] = _bf(_rms(c[:, sl], gconv_ref[:, sl]))


def _proj(x, cs, s0, s1, wts, seq_conv):
    b, s, _ = x.shape
    tm = min(512, s)
    grid = (b, s // tm)
    z_rows = SUBLANES if seq_conv else tm
    full = lambda shape: pl.BlockSpec(shape, lambda i, j: (0,) * len(shape))
    tok = lambda width: pl.BlockSpec((1, tm, width), lambda i, j: (i, j, 0))
    prev = pl.BlockSpec((s0.shape[0] if seq_conv else tm, CONV_WIDTH),
                        lambda i, j: (0 if seq_conv else j, 0))
    out_shapes = (
        jax.ShapeDtypeStruct((b, N_HEADS, s, QK_PAD), jnp.bfloat16),
        jax.ShapeDtypeStruct((b, s, QK_PAD), jnp.bfloat16),
        jax.ShapeDtypeStruct((b, s, KV_RANK), jnp.float32),
        jax.ShapeDtypeStruct((b, s, QK_ROPE), jnp.float32),
        jax.ShapeDtypeStruct((b, s, CONV_WIDTH), jnp.bfloat16),
        jax.ShapeDtypeStruct((b, z_rows, CONV_WIDTH), jnp.float32),
    )
    out_specs = (
        pl.BlockSpec((1, N_HEADS, tm, QK_PAD), lambda i, j: (i, 0, j, 0)),
        tok(QK_PAD), tok(KV_RANK), tok(QK_ROPE), tok(CONV_WIDTH),
        pl.BlockSpec((1, z_rows, CONV_WIDTH), lambda i, j: (i, 0, 0)),
    )
    return pl.pallas_call(
        functools.partial(_proj_kernel, seq_conv, tm),
        grid=grid,
        in_specs=[
            tok(D_MODEL), full((1, D_MODEL)), full((D_MODEL, D_INX)), full((1, Q_RANK)),
            full((Q_RANK, 2 * N_HEADS * QK_NOPE)), full((N_HEADS, QK_NOPE, KV_RANK)),
            full((1, KV_RANK)), full((CONV_K, CONV_WIDTH)),
            pl.BlockSpec((tm, LANES), lambda i, j: (j, 0)), prev, prev,
            full((1, CONV_WIDTH)),
        ],
        out_specs=out_specs,
        out_shape=out_shapes,
        scratch_shapes=[pltpu.VMEM((tm + 2 * SUBLANES, CONV_WIDTH), jnp.float32)],
        compiler_params=_cparams("arbitrary", "arbitrary"),
        name="proj_seq" if seq_conv else "proj_tok",
    )(x, wts["g_mix"], wts["w_in"], wts["g_q"], wts["w_q"], wts["w_uk"], wts["g_kv"],
      wts["conv_w"], cs, s0, s1, wts["g_conv"])


def _attn_out(o, wuv_ref, gattn_ref, hd):
    a = jnp.dot(_bf(o), wuv_ref[hd], preferred_element_type=jnp.float32)
    return _bf(_rms(a, gattn_ref[:, hd * V_DIM:(hd + 1) * V_DIM]))


def _flash_kernel(tq, tk, q_ref, k_ref, wuv_ref, gattn_ref, out_ref, m_ref, l_ref, acc_ref):
    qi = pl.program_id(1)
    ki = pl.program_id(2)
    rows = N_HEADS * tq
    last_k = (qi * tq + tq - 1) // tk

    @pl.when(ki == 0)
    def _():
        m_ref[...] = jnp.full(m_ref.shape, NEG_INF, jnp.float32)
        l_ref[...] = jnp.zeros(l_ref.shape, jnp.float32)
        acc_ref[...] = jnp.zeros(acc_ref.shape, jnp.float32)

    @pl.when(ki <= last_k)
    def _():
        q = q_ref[0].reshape(rows, QK_PAD)
        k = k_ref[0]
        s = lax.dot_general(q, k, _NT, preferred_element_type=jnp.float32)
        qpos = qi * tq + lax.broadcasted_iota(jnp.int32, (N_HEADS, tq, tk), 1).reshape(rows, tk)
        kpos = ki * tk + lax.broadcasted_iota(jnp.int32, (rows, tk), 1)
        s = jnp.where(kpos <= qpos, s, NEG_INF)
        m_prev = m_ref[...]
        m_new = jnp.maximum(m_prev, jnp.max(s, axis=-1, keepdims=True))
        alpha = jnp.exp(m_prev - m_new)
        p = jnp.exp(s - m_new[:, 0:1])
        l_ref[...] = alpha * l_ref[...] + jnp.sum(p, axis=-1, keepdims=True)
        acc_ref[...] = acc_ref[...] * alpha[:, 0:1] + jnp.dot(
            _bf(p), k[:, 0:KV_RANK], preferred_element_type=jnp.float32)
        m_ref[...] = m_new

    @pl.when(ki == last_k)
    def _():
        o = acc_ref[...] / l_ref[:, 0:1]
        for hd in range(N_HEADS):
            out_ref[0, :, hd * V_DIM:(hd + 1) * V_DIM] = _attn_out(
                o[hd * tq:(hd + 1) * tq], wuv_ref, gattn_ref, hd)


def _flash(qcat, kcat, wts):
    b, _, s, _ = qcat.shape
    tq, tk = 256, 512
    rows = N_HEADS * tq
    return pl.pallas_call(
        functools.partial(_flash_kernel, tq, tk),
        grid=(b, s // tq, s // tk),
        in_specs=[
            pl.BlockSpec((1, N_HEADS, tq, QK_PAD), lambda i, q, k: (i, 0, q, 0)),
            pl.BlockSpec((1, tk, QK_PAD),
                         lambda i, q, k: (i, jnp.minimum(k, (q * tq + tq - 1) // tk), 0)),
            pl.BlockSpec((N_HEADS, KV_RANK, V_DIM), lambda i, q, k: (0, 0, 0)),
            pl.BlockSpec((1, ATTN_WIDTH), lambda i, q, k: (0, 0)),
        ],
        out_specs=pl.BlockSpec((1, tq, ATTN_WIDTH), lambda i, q, k: (i, q, 0)),
        out_shape=jax.ShapeDtypeStruct((b, s, ATTN_WIDTH), jnp.bfloat16),
        scratch_shapes=[pltpu.VMEM((rows, LANES), jnp.float32),
                        pltpu.VMEM((rows, LANES), jnp.float32),
                        pltpu.VMEM((rows, KV_RANK), jnp.float32)],
        compiler_params=_cparams("arbitrary", "arbitrary", "arbitrary"),
        name="flash",
    )(qcat, kcat, wts["w_uv"], wts["g_attn"])


PAGES_PER_STEP = 8
Q_ROWS = 16


def _decode_kernel(n_steps, pt_ref, q_ref, knew_ref, *refs):
    ckv_refs = refs[0:PAGES_PER_STEP]
    kr_refs = refs[PAGES_PER_STEP:2 * PAGES_PER_STEP]
    out_ref, m_ref, l_ref, acc_ref = refs[2 * PAGES_PER_STEP:]
    j = pl.program_id(1)

    @pl.when(j == 0)
    def _():
        m_ref[...] = jnp.full(m_ref.shape, NEG_INF, jnp.float32)
        l_ref[...] = jnp.zeros(l_ref.shape, jnp.float32)
        acc_ref[...] = jnp.zeros(acc_ref.shape, jnp.float32)

    q = q_ref[0]
    qlat = q[:, 0:KV_RANK]
    qrope = q[:, KV_RANK:KV_RANK + QK_ROPE]
    vals = [_bf(r[0, 0]) for r in ckv_refs]
    s = jnp.concatenate(
        [lax.dot_general(qlat, v, _NT, preferred_element_type=jnp.float32)
         + lax.dot_general(qrope, _bf(r[0, 0]), _NT, preferred_element_type=jnp.float32)
         for v, r in zip(vals, kr_refs)], axis=-1)
    m_prev = m_ref[...]
    m_new = jnp.maximum(m_prev, jnp.max(s, axis=-1, keepdims=True))
    alpha = jnp.exp(m_prev - m_new)
    p = jnp.exp(s - m_new[:, 0:1])
    l_new = alpha * l_ref[...] + jnp.sum(p, axis=-1, keepdims=True)
    acc = acc_ref[...] * alpha[:, 0:1]
    for i, v in enumerate(vals):
        acc = acc + jnp.dot(_bf(p[:, i * PAGE_SIZE:(i + 1) * PAGE_SIZE]), v,
                            preferred_element_type=jnp.float32)
    m_ref[...] = m_new
    l_ref[...] = l_new
    acc_ref[...] = acc

    @pl.when(j == n_steps - 1)
    def _():
        kn = knew_ref[0].astype(jnp.float32)
        s_new = jnp.sum(q.astype(jnp.float32) * kn, axis=-1, keepdims=True)
        m_fin = jnp.maximum(m_new, s_new)
        a2 = jnp.exp(m_new - m_fin)
        p_new = jnp.exp(s_new - m_fin[:, 0:1])
        l_fin = a2 * l_new + p_new
        acc_fin = acc * a2[:, 0:1] + _bf(p_new).astype(jnp.float32) * kn[:, 0:KV_RANK]
        out_ref[0] = acc_fin / l_fin[:, 0:1]


def _decode(page_table, q, knew, cache_ckv, cache_krope):
    nb, n_pages = page_table.shape
    n_steps = n_pages // PAGES_PER_STEP

    def page_spec(width, i):
        return pl.BlockSpec((1, 1, PAGE_SIZE, width),
                            lambda b, j, pt: (0, pt[b, j * PAGES_PER_STEP + i], 0, 0))

    grid_spec = pltpu.PrefetchScalarGridSpec(
        num_scalar_prefetch=1,
        grid=(nb, n_steps),
        in_specs=[pl.BlockSpec((1, Q_ROWS, QK_PAD), lambda b, j, pt: (b, 0, 0)),
                  pl.BlockSpec((1, 1, QK_PAD), lambda b, j, pt: (b, 0, 0))]
        + [page_spec(KV_RANK, i) for i in range(PAGES_PER_STEP)]
        + [page_spec(QK_ROPE, i) for i in range(PAGES_PER_STEP)],
        out_specs=pl.BlockSpec((1, Q_ROWS, KV_RANK), lambda b, j, pt: (b, 0, 0)),
        scratch_shapes=[pltpu.VMEM((Q_ROWS, LANES), jnp.float32),
                        pltpu.VMEM((Q_ROWS, LANES), jnp.float32),
                        pltpu.VMEM((Q_ROWS, KV_RANK), jnp.float32)],
    )
    return pl.pallas_call(
        functools.partial(_decode_kernel, n_steps),
        grid_spec=grid_spec,
        out_shape=jax.ShapeDtypeStruct((nb, Q_ROWS, KV_RANK), jnp.float32),
        compiler_params=_cparams("arbitrary", "arbitrary"),
        name="decode",
    )(page_table, q, knew, *([cache_ckv] * PAGES_PER_STEP), *([cache_krope] * PAGES_PER_STEP))


def _attn_out_kernel(o_ref, wuv_ref, gattn_ref, out_ref):
    for hd in range(N_HEADS):
        out_ref[:, hd * V_DIM:(hd + 1) * V_DIM] = _attn_out(o_ref[hd], wuv_ref, gattn_ref, hd)


def _attn_out_call(o, wts):
    t = o.shape[1]
    return pl.pallas_call(
        _attn_out_kernel,
        out_shape=jax.ShapeDtypeStruct((t, ATTN_WIDTH), jnp.bfloat16),
        name="attn_out",
    )(o, wts["w_uv"], wts["g_attn"])


def _mix_kernel(attn_ref, conv_ref, x_ref, wo_ref, gffn_ref, wqry_ref, keys_ref,
                xp_ref, xn_ref, sc_ref):
    y = (jnp.dot(attn_ref[...], wo_ref[0:ATTN_WIDTH, :], preferred_element_type=jnp.float32)
         + jnp.dot(conv_ref[...], wo_ref[ATTN_WIDTH:, :], preferred_element_type=jnp.float32))
    xp = x_ref[...] + y
    xp_ref[...] = xp
    xn = _rms(xp, gffn_ref[...])
    xn_ref[...] = xn
    q = jnp.dot(_bf(xn), wqry_ref[...], preferred_element_type=jnp.float32)
    half = PEER_DK // 2
    for g in range(GROUPS):
        sc_ref[g] = lax.dot_general(keys_ref[g % 2], _bf(q[:, g * half:(g + 1) * half]), _NT,
                                    preferred_element_type=jnp.float32)


def _mix(attn, conv, x, wts):
    t = x.shape[0]
    tm = min(256, t)
    full = lambda shape: pl.BlockSpec(shape, lambda i: (0,) * len(shape))
    tok = lambda width: pl.BlockSpec((tm, width), lambda i: (i, 0))
    return pl.pallas_call(
        _mix_kernel,
        grid=(t // tm,),
        in_specs=[tok(ATTN_WIDTH), tok(CONV_WIDTH), tok(D_MODEL), full((D_MODEL, D_MODEL)),
                  full((1, D_MODEL)), full((D_MODEL, PEER_HEADS * PEER_DK)),
                  full((2, PEER_KEYS, PEER_DK // 2))],
        out_specs=(tok(D_MODEL), tok(D_MODEL),
                   pl.BlockSpec((GROUPS, PEER_KEYS, tm), lambda i: (0, 0, i))),
        out_shape=(jax.ShapeDtypeStruct((t, D_MODEL), jnp.float32),
                   jax.ShapeDtypeStruct((t, D_MODEL), jnp.float32),
                   jax.ShapeDtypeStruct((GROUPS, PEER_KEYS, t), jnp.float32)),
        compiler_params=_cparams("arbitrary"),
        name="mix",
    )(attn, conv, x, wts["w_o"], wts["g_ffn"], wts["w_query"], wts["sub_keys"])


def _top_rounds(s, ids, payload, n):
    vals, idxs, pays = [], [], []
    sentinel = jnp.int32(2 ** 30)
    for _ in range(n):
        m = jnp.max(s, axis=0, keepdims=True)
        win = jnp.min(jnp.where(s == m, ids, sentinel), axis=0, keepdims=True)
        hit = ids == win
        vals.append(m)
        idxs.append(win)
        if payload is not None:
            pays.append(jnp.max(jnp.where(hit, payload, -1), axis=0, keepdims=True))
        s = jnp.where(hit, NEG_INF, s)
    cat = lambda xs: jnp.concatenate(xs, axis=0)
    return cat(vals), cat(idxs), (cat(pays) if payload is not None else None)


def _topk_kernel(tt, sc_ref, idx_ref, gate_ref):
    k = PEER_TOPK
    key_ids = lax.broadcasted_iota(jnp.int32, (PEER_KEYS, tt), 0)
    blocks = [(a, a + 1, 0, k) for a in range(2)] + [(a, a + 1, 0, 8) for a in range(2, 8)] + [(8, k, 0, 1)]
    for hd in range(PEER_HEADS):
        v0, i0, _ = _top_rounds(sc_ref[2 * hd], key_ids, None, k)
        v1, i1, _ = _top_rounds(sc_ref[2 * hd + 1], key_ids, None, k)
        cand, cid, eid = [], [], []
        for (a0, a1, b0, b1) in blocks:
            na, nb = a1 - a0, b1 - b0
            if na == 1:
                va, ia, vb, ib = v0[a0:a1], i0[a0:a1], v1[b0:b1], i1[b0:b1]
                rank = a0 * k + b0 + lax.broadcasted_iota(jnp.int32, (nb, tt), 0)
            else:
                va, ia, vb, ib = v0[a0:a1], i0[a0:a1], v1[b0:b1], i1[b0:b1]
                rank = (a0 + lax.broadcasted_iota(jnp.int32, (na, tt), 0)) * k + b0
            cand.append(va + vb)
            eid.append(ia * PEER_KEYS + ib)
            cid.append(rank)
        cand = jnp.concatenate(cand, axis=0)
        cid = jnp.concatenate(cid, axis=0)
        eid = jnp.concatenate(eid, axis=0)
        ts, _, te = _top_rounds(cand, cid, eid, k)
        e = jnp.exp(ts - ts[0:1])
        g = e / jnp.sum(e, axis=0, keepdims=True)
        idx_ref[hd * k:(hd + 1) * k, :] = te * ROWS_PER_EXPERT
        gate_ref[hd * k:(hd + 1) * k, :] = g


def _topk(scores):
    t = scores.shape[2]
    tt = LANES
    return pl.pallas_call(
        functools.partial(_topk_kernel, tt),
        grid=(t // tt,),
        in_specs=[pl.BlockSpec((GROUPS, PEER_KEYS, tt), lambda i: (0, 0, i))],
        out_specs=(pl.BlockSpec((N_SEL, tt), lambda i: (0, i)),
                   pl.BlockSpec((N_SEL, tt), lambda i: (0, i))),
        out_shape=(jax.ShapeDtypeStruct((N_SEL, t), jnp.int32),
                   jax.ShapeDtypeStruct((N_SEL, t), jnp.float32)),
        compiler_params=_cparams("arbitrary"),
        name="topk",
    )(scores)


TOK_BLOCK = 128
TOK_UNROLL = 4


def _split_bf16(x):
    hi = _bf(x)
    lo = _bf(x - hi.astype(jnp.float32))
    return hi, lo


def _gather_row(tab_ref, off):
    return tab_ref[pl.ds(pl.multiple_of(off, ROWS_PER_EXPERT), ROWS_PER_EXPERT), :].astype(jnp.float32)


def _peer_u_kernel(idx_ref, x_ref, gate_ref, tab_ref, out_ref, *p_refs):
    ones = jnp.ones((SUBLANES, LANES), jnp.bfloat16)

    def group(i, carry):
        for u, p_ref in enumerate(p_refs):
            t = i * TOK_UNROLL + u
            xt = x_ref[pl.ds(pl.multiple_of(t * ROWS_PER_EXPERT, ROWS_PER_EXPERT), ROWS_PER_EXPERT), :]
            for k in range(N_SEL):
                p_ref[k * ROWS_PER_EXPERT:(k + 1) * ROWS_PER_EXPERT, :] = (
                    _gather_row(tab_ref, idx_ref[t, k]) * xt)
        for u, p_ref in enumerate(p_refs):
            t = i * TOK_UNROLL + u
            f = p_ref[pl.ds(0, N_SEL, stride=ROWS_PER_EXPERT), :]
            for r in range(1, ROWS_PER_EXPERT):
                f = f + p_ref[pl.ds(r, N_SEL, stride=ROWS_PER_EXPERT), :]
            f_hi, f_lo = _split_bf16(f)
            a = (lax.dot_general(ones, f_hi, _NT, preferred_element_type=jnp.float32)
                 + lax.dot_general(ones, f_lo, _NT, preferred_element_type=jnp.float32))
            out_ref[pl.ds(t, 1), :] = a[0:1]
        return carry

    lax.fori_loop(0, out_ref.shape[0] // TOK_UNROLL, group, 0)
    a = out_ref[...]
    out_ref[...] = gate_ref[...] * (0.5 * a * (1.0 + lax.erf(a * (2.0 ** -0.5))))


def _peer_u(idx, xr, gates, tab):
    t = idx.shape[0]
    tb = min(TOK_BLOCK, t)
    return pl.pallas_call(
        _peer_u_kernel,
        grid=(t // tb,),
        in_specs=[
            pl.BlockSpec((tb, N_SEL), lambda i: (i, 0), memory_space=pltpu.SMEM),
            pl.BlockSpec((tb * ROWS_PER_EXPERT, LANES), lambda i: (i, 0)),
            pl.BlockSpec((tb, N_SEL), lambda i: (i, 0)),
            pl.BlockSpec(memory_space=pltpu.VMEM),
        ],
        out_specs=pl.BlockSpec((tb, N_SEL), lambda i: (i, 0)),
        out_shape=jax.ShapeDtypeStruct((t, N_SEL), jnp.float32),
        scratch_shapes=[pltpu.VMEM((N_SEL * ROWS_PER_EXPERT, LANES), jnp.float32)] * TOK_UNROLL,
        compiler_params=_cparams("arbitrary"),
        name="peer_u",
    )(idx, xr, gates, tab)


def _peer_v_kernel(idx_ref, w_ref, tab_ref, out_ref, wb_ref):
    ones = jnp.ones((LANES, LANES), jnp.bfloat16)
    eye = (lax.broadcasted_iota(jnp.int32, (N_SEL, LANES), 0)
           == lax.broadcasted_iota(jnp.int32, (N_SEL, LANES), 1)).astype(jnp.float32)

    def group(i, carry):
        d = jnp.concatenate(
            [eye * w_ref[pl.ds(i * TOK_UNROLL + u, 1), :] for u in range(TOK_UNROLL)], axis=0)
        d_hi, d_lo = _split_bf16(d)
        wb_ref[...] = (jnp.dot(d_hi, ones, preferred_element_type=jnp.float32)
                       + jnp.dot(d_lo, ones, preferred_element_type=jnp.float32))
        for u in range(TOK_UNROLL):
            t = i * TOK_UNROLL + u
            accs = [jnp.zeros((ROWS_PER_EXPERT, LANES), jnp.float32) for _ in range(4)]
            for k in range(N_SEL):
                w = jnp.broadcast_to(wb_ref[u * N_SEL + k:u * N_SEL + k + 1, :], (ROWS_PER_EXPERT, LANES))
                accs[k % 4] = accs[k % 4] + w * _gather_row(tab_ref, idx_ref[t, k])
            out_ref[pl.ds(pl.multiple_of(t * ROWS_PER_EXPERT, ROWS_PER_EXPERT), ROWS_PER_EXPERT), :] = (
                (accs[0] + accs[1]) + (accs[2] + accs[3]))
        return carry

    lax.fori_loop(0, idx_ref.shape[0] // TOK_UNROLL, group, 0)


def _peer_v(idx, w, tab):
    t = idx.shape[0]
    tb = min(TOK_BLOCK, t)
    return pl.pallas_call(
        _peer_v_kernel,
        grid=(t // tb,),
        in_specs=[
            pl.BlockSpec((tb, N_SEL), lambda i: (i, 0), memory_space=pltpu.SMEM),
            pl.BlockSpec((tb, N_SEL), lambda i: (i, 0)),
            pl.BlockSpec(memory_space=pltpu.VMEM),
        ],
        out_specs=pl.BlockSpec((tb * ROWS_PER_EXPERT, LANES), lambda i: (i, 0)),
        out_shape=jax.ShapeDtypeStruct((t * ROWS_PER_EXPERT, LANES), jnp.float32),
        scratch_shapes=[pltpu.VMEM((TOK_UNROLL * N_SEL, LANES), jnp.float32)],
        compiler_params=_cparams("arbitrary"),
        name="peer_v",
    )(idx, w, tab)


def _final_kernel(xp_ref, o_ref, g_ref, y_ref):
    y_ref[...] = _rms(xp_ref[...] + o_ref[...], g_ref[...])


def _final(xp, o, g):
    t = xp.shape[0]
    tm = min(512, t)
    tok = pl.BlockSpec((tm, D_MODEL), lambda i: (i, 0))
    return pl.pallas_call(
        _final_kernel,
        grid=(t // tm,),
        in_specs=[tok, tok, pl.BlockSpec((1, D_MODEL), lambda i: (0, 0))],
        out_specs=tok,
        out_shape=jax.ShapeDtypeStruct((t, D_MODEL), jnp.float32),
        compiler_params=_cparams("arbitrary"),
        name="final",
    )(xp, o, g)


def _rope_table(pos):
    inv = ROPE_THETA ** (-jnp.arange(0, QK_ROPE, 2, dtype=jnp.float32) / QK_ROPE)
    ang = pos.astype(jnp.float32)[:, None] * inv[None, :]
    c, s = jnp.cos(ang), jnp.sin(ang)
    return jnp.concatenate([c, c, -s, s], axis=-1)


def _swap_halves(w):
    half = w.shape[-1] // 2
    return jnp.concatenate([w[..., half:], w[..., :half]], axis=-1)


def _prepare_weights(g_mix_norm, w_in, g_q, w_uq, g_kv, w_uk, w_uv, conv_w, g_attn_out, g_conv_out,
                     w_o, g_ffn_norm, w_query, sub_keys, expert_u, expert_v):
    o1 = Q_RANK
    o2 = o1 + KV_RANK
    o3 = o2 + QK_ROPE
    w_kr = w_in[:, o2:o3]
    w_inx = jnp.concatenate([w_in[:, :o3], _swap_halves(w_kr), w_in[:, o3:]], axis=1)
    q_nope = w_uq[:, :, :QK_NOPE].reshape(Q_RANK, N_HEADS * QK_NOPE)
    q_rope = w_uq[:, :, QK_NOPE:]
    q_pair = jnp.concatenate([q_rope, _swap_halves(q_rope)], axis=-1).reshape(Q_RANK, N_HEADS * LANES)
    row = lambda g: g.reshape(1, -1)
    table = lambda e: _bf(e).reshape(-1, LANES)
    return {
        "g_mix": row(g_mix_norm), "w_in": _bf(w_inx), "g_q": row(g_q),
        "w_q": _bf(jnp.concatenate([q_nope, q_pair], axis=1)),
        "w_uk": _bf(jnp.transpose(w_uk, (1, 2, 0))), "g_kv": row(g_kv),
        "w_uv": _bf(jnp.transpose(w_uv, (1, 0, 2))), "conv_w": conv_w,
        "g_attn": row(g_attn_out), "g_conv": row(g_conv_out), "w_o": _bf(w_o),
        "g_ffn": row(g_ffn_norm), "w_query": _bf(w_query), "sub_keys": _bf(sub_keys),
        "tab_u": table(expert_u), "tab_v": table(expert_v),
    }


def _channel_mixer(attn, conv, x, wts, g_final):
    t = x.shape[0]
    xp, xn, scores = _mix(attn, conv, x, wts)
    idx_t, gate_t = _topk(scores)
    idx, gates = idx_t.T, gate_t.T
    xr = xn.reshape(t * ROWS_PER_EXPERT, LANES)
    w = _peer_u(idx, xr, gates, wts["tab_u"])
    o = _peer_v(idx, w, wts["tab_v"]).reshape(t, D_MODEL)
    return _final(xp, o, g_final.reshape(1, D_MODEL))


def kernel(x_prompt, x_sample, cache_ckv, cache_krope, state_conv, page_table, g_mix_norm, w_in, g_q, w_uq, g_kv, w_uk, w_uv, conv_w, g_attn_out, g_conv_out, w_o, g_ffn_norm, w_query, sub_keys, expert_u, expert_v, g_final):
    assert w_in.shape[0] == 1, "single-layer step"
    wts = _prepare_weights(g_mix_norm[0], w_in[0], g_q[0], w_uq[0], g_kv[0], w_uk[0], w_uv[0], conv_w[0],
                           g_attn_out[0], g_conv_out[0], w_o[0], g_ffn_norm[0], w_query[0], sub_keys[0],
                           expert_u[0], expert_v[0])
    b, s, _ = x_prompt.shape
    nb, n_pages = page_table.shape
    past = n_pages * PAGE_SIZE

    zero_prev = jnp.zeros((SUBLANES, CONV_WIDTH), jnp.float32)
    qcat, kcat, ckv_p, kr_p, conv_p, tail_p = _proj(
        x_prompt, _rope_table(jnp.arange(s)), zero_prev, zero_prev, wts, True)
    attn_p = _flash(qcat, kcat, wts)
    y_prompt = _channel_mixer(attn_p.reshape(b * s, ATTN_WIDTH), conv_p.reshape(b * s, CONV_WIDTH),
                              x_prompt.reshape(b * s, D_MODEL), wts, g_final).reshape(b, s, D_MODEL)
    new_conv_p = tail_p[:, SUBLANES - (CONV_K - 1):, :]

    xs = x_sample.reshape(1, nb, D_MODEL)
    cs_s = jnp.broadcast_to(_rope_table(jnp.full((1,), past)), (nb, LANES))
    s0, s1 = state_conv[0, :, 0, :], state_conv[0, :, 1, :]
    qcat_s, kcat_s, ckv_s, kr_s, conv_s, z_s = _proj(xs, cs_s, s0, s1, wts, False)
    q_dec = jnp.pad(jnp.transpose(qcat_s[0], (1, 0, 2)), ((0, 0), (0, Q_ROWS - N_HEADS), (0, 0)))
    o_lat = _decode(page_table, q_dec, kcat_s.reshape(nb, 1, QK_PAD), cache_ckv, cache_krope)
    attn_s = _attn_out_call(jnp.transpose(o_lat[:, :N_HEADS], (1, 0, 2)), wts)
    y_sample = _channel_mixer(attn_s, conv_s[0], x_sample.reshape(nb, D_MODEL), wts, g_final)
    new_conv_s = jnp.stack([s1, z_s[0]], axis=1)

    return (y_prompt, y_sample.reshape(nb, 1, D_MODEL),
            ckv_p[None], kr_p[None], new_conv_p[None],
            ckv_s.reshape(1, nb, 1, KV_RANK), kr_s.reshape(1, nb, 1, QK_ROPE), new_conv_s[None])
```

```python
import functools

import jax
import jax.numpy as jnp
import numpy as np
from jax import lax
from jax.experimental import pallas as pl
from jax.experimental.pallas import tpu as pltpu

D_MODEL = 1024
N_HEADS = 4
QK_NOPE = 128
QK_ROPE = 64
V_DIM = 128
Q_RANK = 384
KV_RANK = 256
ATTN_WIDTH = N_HEADS * V_DIM
CONV_WIDTH = D_MODEL - ATTN_WIDTH
CONV_GROUPS = 4
CONV_K = 3
PEER_HEADS = 8
PEER_KEYS = 128
PEER_DK = 256
PEER_TOPK = 16
PAGE_SIZE = 128
ROPE_THETA = 10000.0
EPS = 1e-6
SM_SCALE = (QK_NOPE + QK_ROPE) ** -0.5

LANES = 128
SUBLANES = 8
VMEM_LIMIT = 48 * 1024 * 1024

QK_PAD = 384
N_SEL = PEER_HEADS * PEER_TOPK
ROWS_PER_EXPERT = D_MODEL // LANES
GROUPS = 2 * PEER_HEADS
D_INX = Q_RANK + KV_RANK + 2 * QK_ROPE + 3 * CONV_WIDTH
NEG_INF = float("-inf")

_NT = (((1,), (1,)), ((), ()))


def _cparams(*sem):
    return pltpu.CompilerParams(dimension_semantics=sem, vmem_limit_bytes=VMEM_LIMIT)


def _rms(x, g):
    return x * lax.rsqrt(jnp.mean(x * x, axis=-1, keepdims=True) + EPS) * g


def _bf(x):
    return x.astype(jnp.bfloat16)


def _rope_pair(t, cs):
    r = t * cs
    return r + pltpu.roll(r, QK_ROPE, axis=1)


def _proj_kernel(seq_conv, tm, x_ref, gmix_ref, win_ref, gq_ref, wq_ref, wuk_ref, gkv_ref,
                 convw_ref, cs_ref, s0_ref, s1_ref, gconv_ref,
                 qcat_ref, kcat_ref, ckv_ref, kr_ref, convn_ref, z_ref, zbuf_ref):
    si = pl.program_id(1)
    x = x_ref[0]
    xn = _bf(_rms(x, gmix_ref[...]))
    h = jnp.dot(xn, win_ref[...], preferred_element_type=jnp.float32)
    o1 = Q_RANK
    o2 = o1 + KV_RANK
    o3 = o2 + 2 * QK_ROPE
    o4 = o3 + CONV_WIDTH
    o5 = o4 + CONV_WIDTH
    cs = cs_ref[...]
    lane = lax.broadcasted_iota(jnp.int32, (tm, LANES), 1)
    rope_mask = lane < QK_ROPE

    ckv = _rms(h[:, o1:o2], gkv_ref[...])
    kr = _rope_pair(h[:, o2:o3], cs)
    ckv_ref[0] = ckv
    kr_ref[0] = kr[:, :QK_ROPE]
    kcat_ref[0, :, 0:KV_RANK] = _bf(ckv)
    kcat_ref[0, :, KV_RANK:QK_PAD] = _bf(jnp.where(rope_mask, kr, 0.0))

    cq = _bf(_rms(h[:, 0:o1], gq_ref[...]))
    q = jnp.dot(cq, wq_ref[...], preferred_element_type=jnp.float32)
    for hd in range(N_HEADS):
        qn = _bf(q[:, hd * QK_NOPE:(hd + 1) * QK_NOPE])
        qlat = jnp.dot(qn, wuk_ref[hd], preferred_element_type=jnp.float32)
        base = N_HEADS * QK_NOPE + hd * LANES
        qr = _rope_pair(q[:, base:base + LANES], cs)
        qcat_ref[0, hd, :, 0:KV_RANK] = _bf(qlat * SM_SCALE)
        qcat_ref[0, hd, :, KV_RANK:QK_PAD] = _bf(jnp.where(rope_mask, qr * SM_SCALE, 0.0))

    z = h[:, o4:o5] * h[:, o3:o4]
    zb = h[:, o5:]
    w0 = convw_ref[0:1, :]
    w1 = convw_ref[1:2, :]
    w2 = convw_ref[2:3, :]
    if seq_conv:
        @pl.when(si == 0)
        def _():
            zbuf_ref[0:SUBLANES, :] = jnp.zeros((SUBLANES, CONV_WIDTH), jnp.float32)

        zbuf_ref[SUBLANES:SUBLANES + tm, :] = z
        y = (w2 * z + w1 * zbuf_ref[SUBLANES - 1:SUBLANES - 1 + tm, :]
             + w0 * zbuf_ref[SUBLANES - 2:SUBLANES - 2 + tm, :])
        tail = zbuf_ref[tm:tm + SUBLANES, :]
        zbuf_ref[0:SUBLANES, :] = tail
        z_ref[0] = tail
    else:
        y = w2 * z + w1 * s1_ref[...] + w0 * s0_ref[...]
        z_ref[0] = z
    c = zb * y
    for g in range(CONV_GROUPS):
        sl = slice(g * LANES, (g + 1) * LANES)
        convn_ref[0, :, sl] = _bf(_rms(c[:, sl], gconv_ref[:, sl]))


def _proj(x, cs, s0, s1, wts, seq_conv):
    b, s, _ = x.shape
    tm = min(512, s)
    grid = (b, s // tm)
    z_rows = SUBLANES if seq_conv else tm
    full = lambda shape: pl.BlockSpec(shape, lambda i, j: (0,) * len(shape))
    tok = lambda width: pl.BlockSpec((1, tm, width), lambda i, j: (i, j, 0))
    prev = pl.BlockSpec((s0.shape[0] if seq_conv else tm, CONV_WIDTH),
                        lambda i, j: (0 if seq_conv else j, 0))
    out_shapes = (
        jax.ShapeDtypeStruct((b, N_HEADS, s, QK_PAD), jnp.bfloat16),
        jax.ShapeDtypeStruct((b, s, QK_PAD), jnp.bfloat16),
        jax.ShapeDtypeStruct((b, s, KV_RANK), jnp.float32),
        jax.ShapeDtypeStruct((b, s, QK_ROPE), jnp.float32),
        jax.ShapeDtypeStruct((b, s, CONV_WIDTH), jnp.bfloat16),
        jax.ShapeDtypeStruct((b, z_rows, CONV_WIDTH), jnp.float32),
    )
    out_specs = (
        pl.BlockSpec((1, N_HEADS, tm, QK_PAD), lambda i, j: (i, 0, j, 0)),
        tok(QK_PAD), tok(KV_RANK), tok(QK_ROPE), tok(CONV_WIDTH),
        pl.BlockSpec((1, z_rows, CONV_WIDTH), lambda i, j: (i, 0, 0)),
    )
    return pl.pallas_call(
        functools.partial(_proj_kernel, seq_conv, tm),
        grid=grid,
        in_specs=[
            tok(D_MODEL), full((1, D_MODEL)), full((D_MODEL, D_INX)), full((1, Q_RANK)),
            full((Q_RANK, 2 * N_HEADS * QK_NOPE)), full((N_HEADS, QK_NOPE, KV_RANK)),
            full((1, KV_RANK)), full((CONV_K, CONV_WIDTH)),
            pl.BlockSpec((tm, LANES), lambda i, j: (j, 0)), prev, prev,
            full((1, CONV_WIDTH)),
        ],
        out_specs=out_specs,
        out_shape=out_shapes,
        scratch_shapes=[pltpu.VMEM((tm + 2 * SUBLANES, CONV_WIDTH), jnp.float32)],
        compiler_params=_cparams("arbitrary", "arbitrary"),
        name="proj_seq" if seq_conv else "proj_tok",
    )(x, wts["g_mix"], wts["w_in"], wts["g_q"], wts["w_q"], wts["w_uk"], wts["g_kv"],
      wts["conv_w"], cs, s0, s1, wts["g_conv"])


def _attn_out(o, wuv_ref, gattn_ref, hd):
    a = jnp.dot(_bf(o), wuv_ref[hd], preferred_element_type=jnp.float32)
    return _bf(_rms(a, gattn_ref[:, hd * V_DIM:(hd + 1) * V_DIM]))


def _flash_kernel(tq, tk, q_ref, k_ref, wuv_ref, gattn_ref, out_ref, m_ref, l_ref, acc_ref):
    qi = pl.program_id(1)
    ki = pl.program_id(2)
    rows = N_HEADS * tq
    last_k = (qi * tq + tq - 1) // tk

    @pl.when(ki == 0)
    def _():
        m_ref[...] = jnp.full(m_ref.shape, NEG_INF, jnp.float32)
        l_ref[...] = jnp.zeros(l_ref.shape, jnp.float32)
        acc_ref[...] = jnp.zeros(acc_ref.shape, jnp.float32)

    first_masked = (qi * tq + 1) // tk

    def step(masked):
        q = q_ref[0].reshape(rows, QK_PAD)
        k = k_ref[0]
        s = lax.dot_general(q, k, _NT, preferred_element_type=jnp.float32)
        if masked:
            qpos = qi * tq + lax.broadcasted_iota(jnp.int32, (N_HEADS, tq, tk), 1).reshape(rows, tk)
            kpos = ki * tk + lax.broadcasted_iota(jnp.int32, (rows, tk), 1)
            s = jnp.where(kpos <= qpos, s, NEG_INF)
        m_prev = m_ref[...]
        m_new = jnp.maximum(m_prev, jnp.max(s, axis=-1, keepdims=True))
        alpha = jnp.exp(m_prev - m_new)
        p = jnp.exp(s - m_new[:, 0:1])
        l_ref[...] = alpha * l_ref[...] + jnp.sum(p, axis=-1, keepdims=True)
        acc_ref[...] = acc_ref[...] * alpha[:, 0:1] + jnp.dot(
            _bf(p), k[:, 0:KV_RANK], preferred_element_type=jnp.float32)
        m_ref[...] = m_new

    pl.when(ki < first_masked)(functools.partial(step, False))
    pl.when((ki >= first_masked) & (ki <= last_k))(functools.partial(step, True))

    @pl.when(ki == last_k)
    def _():
        o = acc_ref[...] / l_ref[:, 0:1]
        for hd in range(N_HEADS):
            out_ref[0, :, hd * V_DIM:(hd + 1) * V_DIM] = _attn_out(
                o[hd * tq:(hd + 1) * tq], wuv_ref, gattn_ref, hd)


def _flash(qcat, kcat, wts):
    b, _, s, _ = qcat.shape
    tq, tk = 256, 512
    rows = N_HEADS * tq
    return pl.pallas_call(
        functools.partial(_flash_kernel, tq, tk),
        grid=(b, s // tq, s // tk),
        in_specs=[
            pl.BlockSpec((1, N_HEADS, tq, QK_PAD), lambda i, q, k: (i, 0, q, 0)),
            pl.BlockSpec((1, tk, QK_PAD),
                         lambda i, q, k: (i, jnp.minimum(k, (q * tq + tq - 1) // tk), 0)),
            pl.BlockSpec((N_HEADS, KV_RANK, V_DIM), lambda i, q, k: (0, 0, 0)),
            pl.BlockSpec((1, ATTN_WIDTH), lambda i, q, k: (0, 0)),
        ],
        out_specs=pl.BlockSpec((1, tq, ATTN_WIDTH), lambda i, q, k: (i, q, 0)),
        out_shape=jax.ShapeDtypeStruct((b, s, ATTN_WIDTH), jnp.bfloat16),
        scratch_shapes=[pltpu.VMEM((rows, LANES), jnp.float32),
                        pltpu.VMEM((rows, LANES), jnp.float32),
                        pltpu.VMEM((rows, KV_RANK), jnp.float32)],
        compiler_params=_cparams("arbitrary", "arbitrary", "arbitrary"),
        name="flash",
    )(qcat, kcat, wts["w_uv"], wts["g_attn"])


PAGES_PER_STEP = 16
Q_ROWS = 16


def _decode_kernel(n_steps, pt_ref, q_ref, knew_ref, *refs):
    ckv_refs = refs[0:PAGES_PER_STEP]
    kr_refs = refs[PAGES_PER_STEP:2 * PAGES_PER_STEP]
    out_ref, m_ref, l_ref, acc_ref = refs[2 * PAGES_PER_STEP:]
    j = pl.program_id(1)

    @pl.when(j == 0)
    def _():
        m_ref[...] = jnp.full(m_ref.shape, NEG_INF, jnp.float32)
        l_ref[...] = jnp.zeros(l_ref.shape, jnp.float32)
        acc_ref[...] = jnp.zeros(acc_ref.shape, jnp.float32)

    q = q_ref[0]
    qlat = q[:, 0:KV_RANK]
    qrope = q[:, KV_RANK:KV_RANK + QK_ROPE]
    vals = [_bf(r[0, 0]) for r in ckv_refs]
    s = jnp.concatenate(
        [lax.dot_general(qlat, v, _NT, preferred_element_type=jnp.float32)
         + jnp.dot(qrope, _bf(r[0, 0]), preferred_element_type=jnp.float32)
         for v, r in zip(vals, kr_refs)], axis=-1)
    m_prev = m_ref[...]
    m_new = jnp.maximum(m_prev, jnp.max(s, axis=-1, keepdims=True))
    alpha = jnp.exp(m_prev - m_new)
    p = jnp.exp(s - m_new[:, 0:1])
    l_new = alpha * l_ref[...] + jnp.sum(p, axis=-1, keepdims=True)
    acc = acc_ref[...] * alpha[:, 0:1]
    for i, v in enumerate(vals):
        acc = acc + jnp.dot(_bf(p[:, i * PAGE_SIZE:(i + 1) * PAGE_SIZE]), v,
                            preferred_element_type=jnp.float32)
    m_ref[...] = m_new
    l_ref[...] = l_new
    acc_ref[...] = acc

    @pl.when(j == n_steps - 1)
    def _():
        kn = knew_ref[0].astype(jnp.float32)
        s_new = jnp.sum(q.astype(jnp.float32) * kn, axis=-1, keepdims=True)
        m_fin = jnp.maximum(m_new, s_new)
        a2 = jnp.exp(m_new - m_fin)
        p_new = jnp.exp(s_new - m_fin[:, 0:1])
        l_fin = a2 * l_new + p_new
        acc_fin = acc * a2[:, 0:1] + _bf(p_new).astype(jnp.float32) * kn[:, 0:KV_RANK]
        out_ref[0] = acc_fin / l_fin[:, 0:1]


def _decode(page_table, q, knew, cache_ckv, cache_krope_t):
    nb, n_pages = page_table.shape
    n_steps = n_pages // PAGES_PER_STEP

    def page_spec(rows, width, i):
        return pl.BlockSpec((1, 1, rows, width),
                            lambda b, j, pt: (0, pt[b, j * PAGES_PER_STEP + i], 0, 0))

    grid_spec = pltpu.PrefetchScalarGridSpec(
        num_scalar_prefetch=1,
        grid=(nb, n_steps),
        in_specs=[pl.BlockSpec((1, Q_ROWS, QK_PAD), lambda b, j, pt: (b, 0, 0)),
                  pl.BlockSpec((1, 1, QK_PAD), lambda b, j, pt: (b, 0, 0))]
        + [page_spec(PAGE_SIZE, KV_RANK, i) for i in range(PAGES_PER_STEP)]
        + [page_spec(QK_ROPE, PAGE_SIZE, i) for i in range(PAGES_PER_STEP)],
        out_specs=pl.BlockSpec((1, Q_ROWS, KV_RANK), lambda b, j, pt: (b, 0, 0)),
        scratch_shapes=[pltpu.VMEM((Q_ROWS, LANES), jnp.float32),
                        pltpu.VMEM((Q_ROWS, LANES), jnp.float32),
                        pltpu.VMEM((Q_ROWS, KV_RANK), jnp.float32)],
    )
    return pl.pallas_call(
        functools.partial(_decode_kernel, n_steps),
        grid_spec=grid_spec,
        out_shape=jax.ShapeDtypeStruct((nb, Q_ROWS, KV_RANK), jnp.float32),
        compiler_params=_cparams("arbitrary", "arbitrary"),
        name="decode",
    )(page_table, q, knew, *([cache_ckv] * PAGES_PER_STEP), *([cache_krope_t] * PAGES_PER_STEP))


def _attn_out_kernel(o_ref, wuv_ref, gattn_ref, out_ref):
    for hd in range(N_HEADS):
        out_ref[:, hd * V_DIM:(hd + 1) * V_DIM] = _attn_out(o_ref[hd], wuv_ref, gattn_ref, hd)


def _attn_out_call(o, wts):
    t = o.shape[1]
    return pl.pallas_call(
        _attn_out_kernel,
        out_shape=jax.ShapeDtypeStruct((t, ATTN_WIDTH), jnp.bfloat16),
        name="attn_out",
    )(o, wts["w_uv"], wts["g_attn"])


def _mix_kernel(attn_ref, conv_ref, x_ref, wo_ref, gffn_ref, wqry_ref, keys_ref,
                xp_ref, xn_ref, sc_ref):
    y = (jnp.dot(attn_ref[...], wo_ref[0:ATTN_WIDTH, :], preferred_element_type=jnp.float32)
         + jnp.dot(conv_ref[...], wo_ref[ATTN_WIDTH:, :], preferred_element_type=jnp.float32))
    xp = x_ref[...] + y
    xp_ref[...] = xp
    xn = _rms(xp, gffn_ref[...])
    xn_ref[...] = xn
    q = jnp.dot(_bf(xn), wqry_ref[...], preferred_element_type=jnp.float32)
    half = PEER_DK // 2
    for g in range(GROUPS):
        sc_ref[g] = lax.dot_general(keys_ref[g % 2], _bf(q[:, g * half:(g + 1) * half]), _NT,
                                    preferred_element_type=jnp.float32)


def _mix(attn, conv, x, wts):
    t = x.shape[0]
    tm = min(256, t)
    full = lambda shape: pl.BlockSpec(shape, lambda i: (0,) * len(shape))
    tok = lambda width: pl.BlockSpec((tm, width), lambda i: (i, 0))
    return pl.pallas_call(
        _mix_kernel,
        grid=(t // tm,),
        in_specs=[tok(ATTN_WIDTH), tok(CONV_WIDTH), tok(D_MODEL), full((D_MODEL, D_MODEL)),
                  full((1, D_MODEL)), full((D_MODEL, PEER_HEADS * PEER_DK)),
                  full((2, PEER_KEYS, PEER_DK // 2))],
        out_specs=(tok(D_MODEL), tok(D_MODEL),
                   pl.BlockSpec((GROUPS, PEER_KEYS, tm), lambda i: (0, 0, i))),
        out_shape=(jax.ShapeDtypeStruct((t, D_MODEL), jnp.float32),
                   jax.ShapeDtypeStruct((t, D_MODEL), jnp.float32),
                   jax.ShapeDtypeStruct((GROUPS, PEER_KEYS, t), jnp.float32)),
        compiler_params=_cparams("arbitrary"),
        name="mix",
    )(attn, conv, x, wts["w_o"], wts["g_ffn"], wts["w_query"], wts["sub_keys"])


def _top_rounds(s, ids, payload, n):
    vals, idxs, pays = [], [], []
    sentinel = jnp.int32(2 ** 30)
    for _ in range(n):
        m = jnp.max(s, axis=0, keepdims=True)
        win = jnp.min(jnp.where(s == m, ids, sentinel), axis=0, keepdims=True)
        hit = ids == win
        vals.append(m)
        idxs.append(win)
        if payload is not None:
            pays.append(jnp.max(jnp.where(hit, payload, -1), axis=0, keepdims=True))
        s = jnp.where(hit, NEG_INF, s)
    cat = lambda xs: jnp.concatenate(xs, axis=0)
    return cat(vals), cat(idxs), (cat(pays) if payload is not None else None)


def _topk_kernel(tt, sc_ref, idx_ref, gate_ref):
    k = PEER_TOPK
    key_ids = lax.broadcasted_iota(jnp.int32, (PEER_KEYS, tt), 0)
    blocks = [(a, a + 1, 0, k) for a in range(2)] + [(a, a + 1, 0, 8) for a in range(2, 8)] + [(8, k, 0, 1)]
    for hd in range(PEER_HEADS):
        v0, i0, _ = _top_rounds(sc_ref[2 * hd], key_ids, None, k)
        v1, i1, _ = _top_rounds(sc_ref[2 * hd + 1], key_ids, None, k)
        cand, cid, eid = [], [], []
        for (a0, a1, b0, b1) in blocks:
            na, nb = a1 - a0, b1 - b0
            if na == 1:
                va, ia, vb, ib = v0[a0:a1], i0[a0:a1], v1[b0:b1], i1[b0:b1]
                rank = a0 * k + b0 + lax.broadcasted_iota(jnp.int32, (nb, tt), 0)
            else:
                va, ia, vb, ib = v0[a0:a1], i0[a0:a1], v1[b0:b1], i1[b0:b1]
                rank = (a0 + lax.broadcasted_iota(jnp.int32, (na, tt), 0)) * k + b0
            cand.append(va + vb)
            eid.append(ia * PEER_KEYS + ib)
            cid.append(rank)
        cand = jnp.concatenate(cand, axis=0)
        cid = jnp.concatenate(cid, axis=0)
        eid = jnp.concatenate(eid, axis=0)
        ts, _, te = _top_rounds(cand, cid, eid, k)
        e = jnp.exp(ts - ts[0:1])
        g = e / jnp.sum(e, axis=0, keepdims=True)
        idx_ref[hd * k:(hd + 1) * k, :] = te * PACKED_ROWS
        gate_ref[hd * k:(hd + 1) * k, :] = g


def _topk(scores):
    t = scores.shape[2]
    tt = LANES
    return pl.pallas_call(
        functools.partial(_topk_kernel, tt),
        grid=(t // tt,),
        in_specs=[pl.BlockSpec((GROUPS, PEER_KEYS, tt), lambda i: (0, 0, i))],
        out_specs=(pl.BlockSpec((N_SEL, tt), lambda i: (0, i)),
                   pl.BlockSpec((N_SEL, tt), lambda i: (0, i))),
        out_shape=(jax.ShapeDtypeStruct((N_SEL, t), jnp.int32),
                   jax.ShapeDtypeStruct((N_SEL, t), jnp.float32)),
        compiler_params=_cparams("arbitrary"),
        name="topk",
    )(scores)


TOK_BLOCK = 128
TOK_GROUP = 8
PACKED_ROWS = ROWS_PER_EXPERT // 2
PACK_BLOCK = 8192


def _split_bf16(x):
    hi = _bf(x)
    lo = _bf(x - hi.astype(jnp.float32))
    return hi, lo


def _pack_kernel(x_ref, o_ref):
    o_ref[...] = pltpu.bitcast(_bf(x_ref[...]), jnp.int32)


def _pack_table(tab):
    rows = tab.size // LANES
    return pl.pallas_call(
        _pack_kernel,
        grid=(rows // PACK_BLOCK,),
        in_specs=[pl.BlockSpec((PACK_BLOCK, LANES), lambda i: (i, 0))],
        out_specs=pl.BlockSpec((PACK_BLOCK // 2, LANES), lambda i: (i, 0)),
        out_shape=jax.ShapeDtypeStruct((rows // 2, LANES), jnp.int32),
        compiler_params=_cparams("arbitrary"),
        name="pack_table",
    )(tab.reshape(rows, LANES))


def _gather_tile(tab_ref, off):
    w = tab_ref[pl.ds(pl.multiple_of(off, PACKED_ROWS), PACKED_ROWS), :]
    return pltpu.bitcast(w, jnp.bfloat16).astype(jnp.float32)


def _token_tile(xg, tl):
    return jnp.concatenate(
        [xg[tl:tl + 1, r * LANES:(r + 1) * LANES] for r in range(ROWS_PER_EXPERT)], axis=0)


def _staged_groups(idx_ref, slots, sem, process):
    n_groups = idx_ref.shape[0] // TOK_GROUP

    def copy(g, s):
        rows = pl.ds(pl.multiple_of(g * TOK_GROUP, TOK_GROUP), TOK_GROUP)
        return pltpu.make_async_copy(idx_ref.at[rows, :], slots[s], sem.at[s])

    copy(0, 0).start()

    def pair(i, carry):
        for s in range(2):
            g = 2 * i + s
            copy(g, s).wait()

            @pl.when(g + 1 < n_groups)
            def _():
                copy(g + 1, 1 - s).start()

            process(g, s, slots[s])
        return carry

    lax.fori_loop(0, n_groups // 2, pair, 0)
    return n_groups


FOLD_BLOCK = SUBLANES
P_PITCH = ROWS_PER_EXPERT + 1


def _peer_u_kernel(idx_ref, x_ref, gate_ref, tab_ref, out_ref, slot0, slot1, sem, f0_ref, f1_ref, *p_refs):
    ones = jnp.ones((SUBLANES, LANES), jnp.bfloat16)
    sub = lax.broadcasted_iota(jnp.int32, (SUBLANES, LANES), 0)
    f_refs = (f0_ref, f1_ref)

    def lane_fold(f_ref, tl, rows):
        f_hi, f_lo = _split_bf16(f_ref[tl * N_SEL:(tl + 1) * N_SEL, :])
        a = (lax.dot_general(ones, f_hi, _NT, preferred_element_type=jnp.float32)
             + lax.dot_general(ones, f_lo, _NT, preferred_element_type=jnp.float32))
        return a if rows is None else jnp.where(sub == tl, a, rows)

    def store_rows(group, rows):
        out_ref[pl.ds(pl.multiple_of(group * TOK_GROUP, TOK_GROUP), TOK_GROUP), :] = rows

    def process(g, s, slot):
        xg = x_ref[pl.ds(pl.multiple_of(g * TOK_GROUP, TOK_GROUP), TOK_GROUP), :]
        rows = None
        for tl in range(TOK_GROUP + 1):
            if tl < TOK_GROUP:
                xt = _token_tile(xg, tl)
                rows = lane_fold(f_refs[1 - s], tl, rows)
            for kb in range(N_SEL // FOLD_BLOCK):
                if tl < TOK_GROUP:
                    for k in range(kb * FOLD_BLOCK, (kb + 1) * FOLD_BLOCK):
                        p_refs[tl][k * P_PITCH:k * P_PITCH + ROWS_PER_EXPERT, :] = (
                            _gather_tile(tab_ref, slot[tl, k]) * xt)
                if tl >= 1:
                    p_ref = p_refs[tl - 1]
                    first = kb * FOLD_BLOCK * P_PITCH
                    f = p_ref[pl.ds(first, FOLD_BLOCK, stride=P_PITCH), :]
                    for r in range(1, ROWS_PER_EXPERT):
                        f = f + p_ref[pl.ds(first + r, FOLD_BLOCK, stride=P_PITCH), :]
                    row = (tl - 1) * N_SEL + kb * FOLD_BLOCK
                    f_refs[s][row:row + FOLD_BLOCK, :] = f
        store_rows(jnp.maximum(g - 1, 0), rows)

    f1_ref[...] = jnp.zeros(f1_ref.shape, jnp.float32)
    n_groups = _staged_groups(idx_ref, (slot0, slot1), sem, process)
    rows = None
    for tl in range(TOK_GROUP):
        rows = lane_fold(f_refs[(n_groups - 1) % 2], tl, rows)
    store_rows(n_groups - 1, rows)
    a = out_ref[...]
    out_ref[...] = gate_ref[...] * (0.5 * a * (1.0 + lax.erf(a * (2.0 ** -0.5))))


def _stage_scratch():
    return [pltpu.SMEM((TOK_GROUP, N_SEL), jnp.int32), pltpu.SMEM((TOK_GROUP, N_SEL), jnp.int32),
            pltpu.SemaphoreType.DMA((2,))]


def _peer_u(idx, xn, gates, tab):
    t = idx.shape[0]
    tb = min(TOK_BLOCK, t)
    sel = pl.BlockSpec((tb, N_SEL), lambda i: (i, 0))
    return pl.pallas_call(
        _peer_u_kernel,
        grid=(t // tb,),
        in_specs=[sel, pl.BlockSpec((tb, D_MODEL), lambda i: (i, 0)), sel,
                  pl.BlockSpec(memory_space=pltpu.VMEM)],
        out_specs=sel,
        out_shape=jax.ShapeDtypeStruct((t, N_SEL), jnp.float32),
        scratch_shapes=_stage_scratch()
        + [pltpu.VMEM((TOK_GROUP * N_SEL, LANES), jnp.float32)] * 2
        + [pltpu.VMEM((N_SEL * P_PITCH, LANES), jnp.float32)] * TOK_GROUP,
        compiler_params=_cparams("arbitrary"),
        name="peer_u",
    )(idx, xn, gates, tab)


def _peer_v_kernel(idx_ref, w_ref, tab_ref, out_ref, slot0, slot1, sem, wb0_ref, wb1_ref, acc_ref):
    ones = jnp.ones((LANES, LANES), jnp.bfloat16)
    eye = (lax.broadcasted_iota(jnp.int32, (N_SEL, LANES), 0)
           == lax.broadcasted_iota(jnp.int32, (N_SEL, LANES), 1)).astype(jnp.float32)
    wb_refs = (wb0_ref, wb1_ref)
    last_group = idx_ref.shape[0] // TOK_GROUP - 1

    def group_weights(group):
        return w_ref[pl.ds(pl.multiple_of(group * TOK_GROUP, TOK_GROUP), TOK_GROUP), :]

    def broadcast_weights(wb_ref, wg, tl):
        d_hi, d_lo = _split_bf16(eye * wg[tl:tl + 1, :])
        wb_ref[tl * N_SEL:(tl + 1) * N_SEL, :] = (
            jnp.dot(d_hi, ones, preferred_element_type=jnp.float32)
            + jnp.dot(d_lo, ones, preferred_element_type=jnp.float32))

    def process(g, s, slot):
        wg_next = group_weights(jnp.minimum(g + 1, last_group))
        wb_ref = wb_refs[s]
        for tl in range(TOK_GROUP):
            broadcast_weights(wb_refs[1 - s], wg_next, tl)
            acc = acc_ref.at[tl * ROWS_PER_EXPERT:(tl + 1) * ROWS_PER_EXPERT, :]
            for kb in range(N_SEL // FOLD_BLOCK):
                part = None
                for k in range(kb * FOLD_BLOCK, (kb + 1) * FOLD_BLOCK):
                    row = tl * N_SEL + k
                    w = jnp.broadcast_to(wb_ref[row:row + 1, :], (ROWS_PER_EXPERT, LANES))
                    term = w * _gather_tile(tab_ref, slot[tl, k])
                    part = term if part is None else part + term
                acc[...] = part if kb == 0 else acc[...] + part
        tiles = [acc_ref[tl * ROWS_PER_EXPERT:(tl + 1) * ROWS_PER_EXPERT, :] for tl in range(TOK_GROUP)]
        base = pl.multiple_of(g * TOK_GROUP, TOK_GROUP)
        for r in range(ROWS_PER_EXPERT):
            out_ref[pl.ds(base, TOK_GROUP), r * LANES:(r + 1) * LANES] = jnp.concatenate(
                [tile[r:r + 1, :] for tile in tiles], axis=0)

    wg0 = group_weights(0)
    for tl in range(TOK_GROUP):
        broadcast_weights(wb0_ref, wg0, tl)
    _staged_groups(idx_ref, (slot0, slot1), sem, process)


def _peer_v(idx, w, tab):
    t = idx.shape[0]
    tb = min(TOK_BLOCK, t)
    sel = pl.BlockSpec((tb, N_SEL), lambda i: (i, 0))
    return pl.pallas_call(
        _peer_v_kernel,
        grid=(t // tb,),
        in_specs=[sel, sel, pl.BlockSpec(memory_space=pltpu.VMEM)],
        out_specs=pl.BlockSpec((tb, D_MODEL), lambda i: (i, 0)),
        out_shape=jax.ShapeDtypeStruct((t, D_MODEL), jnp.float32),
        scratch_shapes=_stage_scratch() + [pltpu.VMEM((TOK_GROUP * N_SEL, LANES), jnp.float32)] * 2
        + [pltpu.VMEM((TOK_GROUP * ROWS_PER_EXPERT, LANES), jnp.float32)],
        compiler_params=_cparams("arbitrary"),
        name="peer_v",
    )(idx, w, tab)


def _final_kernel(xp_ref, o_ref, g_ref, y_ref):
    y_ref[...] = _rms(xp_ref[...] + o_ref[...], g_ref[...])


def _final(xp, o, g):
    t = xp.shape[0]
    tm = min(512, t)
    tok = pl.BlockSpec((tm, D_MODEL), lambda i: (i, 0))
    return pl.pallas_call(
        _final_kernel,
        grid=(t // tm,),
        in_specs=[tok, tok, pl.BlockSpec((1, D_MODEL), lambda i: (0, 0))],
        out_specs=tok,
        out_shape=jax.ShapeDtypeStruct((t, D_MODEL), jnp.float32),
        compiler_params=_cparams("arbitrary"),
        name="final",
    )(xp, o, g)


def _rope_table(pos):
    inv = ROPE_THETA ** (-jnp.arange(0, QK_ROPE, 2, dtype=jnp.float32) / QK_ROPE)
    ang = pos.astype(jnp.float32)[:, None] * inv[None, :]
    c, s = jnp.cos(ang), jnp.sin(ang)
    return jnp.concatenate([c, c, -s, s], axis=-1)


def _swap_halves(w):
    half = w.shape[-1] // 2
    return jnp.concatenate([w[..., half:], w[..., :half]], axis=-1)


def _prepare_weights(g_mix_norm, w_in, g_q, w_uq, g_kv, w_uk, w_uv, conv_w, g_attn_out, g_conv_out,
                     w_o, g_ffn_norm, w_query, sub_keys, expert_u, expert_v):
    o1 = Q_RANK
    o2 = o1 + KV_RANK
    o3 = o2 + QK_ROPE
    w_kr = w_in[:, o2:o3]
    w_inx = jnp.concatenate([w_in[:, :o3], _swap_halves(w_kr), w_in[:, o3:]], axis=1)
    q_nope = w_uq[:, :, :QK_NOPE].reshape(Q_RANK, N_HEADS * QK_NOPE)
    q_rope = w_uq[:, :, QK_NOPE:]
    q_pair = jnp.concatenate([q_rope, _swap_halves(q_rope)], axis=-1).reshape(Q_RANK, N_HEADS * LANES)
    row = lambda g: g.reshape(1, -1)
    return {
        "g_mix": row(g_mix_norm), "w_in": _bf(w_inx), "g_q": row(g_q),
        "w_q": _bf(jnp.concatenate([q_nope, q_pair], axis=1)),
        "w_uk": _bf(jnp.transpose(w_uk, (1, 2, 0))), "g_kv": row(g_kv),
        "w_uv": _bf(jnp.transpose(w_uv, (1, 0, 2))), "conv_w": conv_w,
        "g_attn": row(g_attn_out), "g_conv": row(g_conv_out), "w_o": _bf(w_o),
        "g_ffn": row(g_ffn_norm), "w_query": _bf(w_query), "sub_keys": _bf(sub_keys),
        "tab_u": _pack_table(expert_u), "tab_v": _pack_table(expert_v),
    }


def _channel_mixer(attn, conv, x, wts, g_final):
    t = x.shape[0]
    xp, xn, scores = _mix(attn, conv, x, wts)
    idx_t, gate_t = _topk(scores)
    idx, gates = idx_t.T, gate_t.T
    w = _peer_u(idx, xn, gates, wts["tab_u"])
    o = _peer_v(idx, w, wts["tab_v"])
    return _final(xp, o, g_final.reshape(1, D_MODEL))


def kernel(x_prompt, x_sample, cache_ckv, cache_krope, state_conv, page_table, g_mix_norm, w_in, g_q, w_uq, g_kv, w_uk, w_uv, conv_w, g_attn_out, g_conv_out, w_o, g_ffn_norm, w_query, sub_keys, expert_u, expert_v, g_final):
    assert w_in.shape[0] == 1, "single-layer step"
    wts = _prepare_weights(g_mix_norm[0], w_in[0], g_q[0], w_uq[0], g_kv[0], w_uk[0], w_uv[0], conv_w[0],
                           g_attn_out[0], g_conv_out[0], w_o[0], g_ffn_norm[0], w_query[0], sub_keys[0],
                           expert_u[0], expert_v[0])
    b, s, _ = x_prompt.shape
    nb, n_pages = page_table.shape
    past = n_pages * PAGE_SIZE

    zero_prev = jnp.zeros((SUBLANES, CONV_WIDTH), jnp.float32)
    qcat, kcat, ckv_p, kr_p, conv_p, tail_p = _proj(
        x_prompt, _rope_table(jnp.arange(s)), zero_prev, zero_prev, wts, True)
    attn_p = _flash(qcat, kcat, wts)
    y_prompt = _channel_mixer(attn_p.reshape(b * s, ATTN_WIDTH), conv_p.reshape(b * s, CONV_WIDTH),
                              x_prompt.reshape(b * s, D_MODEL), wts, g_final).reshape(b, s, D_MODEL)
    new_conv_p = tail_p[:, SUBLANES - (CONV_K - 1):, :]

    xs = x_sample.reshape(1, nb, D_MODEL)
    cs_s = jnp.broadcast_to(_rope_table(jnp.full((1,), past)), (nb, LANES))
    s0, s1 = state_conv[0, :, 0, :], state_conv[0, :, 1, :]
    qcat_s, kcat_s, ckv_s, kr_s, conv_s, z_s = _proj(xs, cs_s, s0, s1, wts, False)
    q_dec = jnp.pad(jnp.transpose(qcat_s[0], (1, 0, 2)), ((0, 0), (0, Q_ROWS - N_HEADS), (0, 0)))
    o_lat = _decode(page_table, q_dec, kcat_s.reshape(nb, 1, QK_PAD), cache_ckv,
                    jnp.swapaxes(cache_krope, 2, 3))
    attn_s = _attn_out_call(jnp.transpose(o_lat[:, :N_HEADS], (1, 0, 2)), wts)
    y_sample = _channel_mixer(attn_s, conv_s[0], x_sample.reshape(nb, D_MODEL), wts, g_final)
    new_conv_s = jnp.stack([s1, z_s[0]], axis=1)

    return (y_prompt, y_sample.reshape(nb, 1, D_MODEL),
            ckv_p[None], kr_p[None], new_conv_p[None],
            ckv_s.reshape(1, nb, 1, KV_RANK), kr_s.reshape(1, nb, 1, QK_ROPE), new_conv_s[None])
```

```python
import functools

import jax
import jax.numpy as jnp
import numpy as np
from jax import lax
from jax.experimental import pallas as pl
from jax.experimental.pallas import tpu as pltpu

D_MODEL = 1024
N_HEADS = 4
QK_NOPE = 128
QK_ROPE = 64
V_DIM = 128
Q_RANK = 384
KV_RANK = 256
ATTN_WIDTH = N_HEADS * V_DIM
CONV_WIDTH = D_MODEL - ATTN_WIDTH
CONV_GROUPS = 4
CONV_K = 3
PEER_HEADS = 8
PEER_KEYS = 128
PEER_DK = 256
PEER_TOPK = 16
PAGE_SIZE = 128
ROPE_THETA = 10000.0
EPS = 1e-6
SM_SCALE = (QK_NOPE + QK_ROPE) ** -0.5

LANES = 128
SUBLANES = 8
VMEM_LIMIT = 48 * 1024 * 1024

QK_PAD = 384
N_SEL = PEER_HEADS * PEER_TOPK
ROWS_PER_EXPERT = D_MODEL // LANES
GROUPS = 2 * PEER_HEADS
D_INX = Q_RANK + KV_RANK + 2 * QK_ROPE + 3 * CONV_WIDTH
NEG_INF = float("-inf")

_NT = (((1,), (1,)), ((), ()))


def _cparams(*sem):
    return pltpu.CompilerParams(dimension_semantics=sem, vmem_limit_bytes=VMEM_LIMIT)


def _rms(x, g):
    return x * lax.rsqrt(jnp.mean(x * x, axis=-1, keepdims=True) + EPS) * g


def _bf(x):
    return x.astype(jnp.bfloat16)


def _rope_pair(t, cs):
    r = t * cs
    return r + pltpu.roll(r, QK_ROPE, axis=1)


def _proj_kernel(seq_conv, tm, x_ref, gmix_ref, win_ref, gq_ref, wq_ref, wuk_ref, gkv_ref,
                 convw_ref, cs_ref, s0_ref, s1_ref, gconv_ref,
                 qcat_ref, kcat_ref, ckv_ref, kr_ref, convn_ref, z_ref, zbuf_ref):
    si = pl.program_id(1)
    x = x_ref[0]
    xn = _bf(_rms(x, gmix_ref[...]))
    h = jnp.dot(xn, win_ref[...], preferred_element_type=jnp.float32)
    o1 = Q_RANK
    o2 = o1 + KV_RANK
    o3 = o2 + 2 * QK_ROPE
    o4 = o3 + CONV_WIDTH
    o5 = o4 + CONV_WIDTH
    cs = cs_ref[...]
    lane = lax.broadcasted_iota(jnp.int32, (tm, LANES), 1)
    rope_mask = lane < QK_ROPE

    ckv = _rms(h[:, o1:o2], gkv_ref[...])
    kr = _rope_pair(h[:, o2:o3], cs)
    ckv_ref[0] = ckv
    kr_ref[0] = kr[:, :QK_ROPE]
    kcat_ref[0, :, 0:KV_RANK] = _bf(ckv)
    kcat_ref[0, :, KV_RANK:QK_PAD] = _bf(jnp.where(rope_mask, kr, 0.0))

    cq = _bf(_rms(h[:, 0:o1], gq_ref[...]))
    q = jnp.dot(cq, wq_ref[...], preferred_element_type=jnp.float32)
    for hd in range(N_HEADS):
        qn = _bf(q[:, hd * QK_NOPE:(hd + 1) * QK_NOPE])
        qlat = jnp.dot(qn, wuk_ref[hd], preferred_element_type=jnp.float32)
        base = N_HEADS * QK_NOPE + hd * LANES
        qr = _rope_pair(q[:, base:base + LANES], cs)
        qcat_ref[0, hd, :, 0:KV_RANK] = _bf(qlat * SM_SCALE)
        qcat_ref[0, hd, :, KV_RANK:QK_PAD] = _bf(jnp.where(rope_mask, qr * SM_SCALE, 0.0))

    z = h[:, o4:o5] * h[:, o3:o4]
    zb = h[:, o5:]
    w0 = convw_ref[0:1, :]
    w1 = convw_ref[1:2, :]
    w2 = convw_ref[2:3, :]
    if seq_conv:
        @pl.when(si == 0)
        def _():
            zbuf_ref[0:SUBLANES, :] = jnp.zeros((SUBLANES, CONV_WIDTH), jnp.float32)

        zbuf_ref[SUBLANES:SUBLANES + tm, :] = z
        y = (w2 * z + w1 * zbuf_ref[SUBLANES - 1:SUBLANES - 1 + tm, :]
             + w0 * zbuf_ref[SUBLANES - 2:SUBLANES - 2 + tm, :])
        tail = zbuf_ref[tm:tm + SUBLANES, :]
        zbuf_ref[0:SUBLANES, :] = tail
        z_ref[0] = tail
    else:
        y = w2 * z + w1 * s1_ref[...] + w0 * s0_ref[...]
        z_ref[0] = z
    c = zb * y
    for g in range(CONV_GROUPS):
        sl = slice(g * LANES, (g + 1) * LANES)
        convn_ref[0, :, sl] = _bf(_rms(c[:, sl], gconv_ref[:, sl]))


def _proj(x, cs, s0, s1, wts, seq_conv):
    b, s, _ = x.shape
    tm = min(512, s)
    grid = (b, s // tm)
    z_rows = SUBLANES if seq_conv else tm
    full = lambda shape: pl.BlockSpec(shape, lambda i, j: (0,) * len(shape))
    tok = lambda width: pl.BlockSpec((1, tm, width), lambda i, j: (i, j, 0))
    prev = pl.BlockSpec((s0.shape[0] if seq_conv else tm, CONV_WIDTH),
                        lambda i, j: (0 if seq_conv else j, 0))
    out_shapes = (
        jax.ShapeDtypeStruct((b, N_HEADS, s, QK_PAD), jnp.bfloat16),
        jax.ShapeDtypeStruct((b, s, QK_PAD), jnp.bfloat16),
        jax.ShapeDtypeStruct((b, s, KV_RANK), jnp.float32),
        jax.ShapeDtypeStruct((b, s, QK_ROPE), jnp.float32),
        jax.ShapeDtypeStruct((b, s, CONV_WIDTH), jnp.bfloat16),
        jax.ShapeDtypeStruct((b, z_rows, CONV_WIDTH), jnp.float32),
    )
    out_specs = (
        pl.BlockSpec((1, N_HEADS, tm, QK_PAD), lambda i, j: (i, 0, j, 0)),
        tok(QK_PAD), tok(KV_RANK), tok(QK_ROPE), tok(CONV_WIDTH),
        pl.BlockSpec((1, z_rows, CONV_WIDTH), lambda i, j: (i, 0, 0)),
    )
    return pl.pallas_call(
        functools.partial(_proj_kernel, seq_conv, tm),
        grid=grid,
        in_specs=[
            tok(D_MODEL), full((1, D_MODEL)), full((D_MODEL, D_INX)), full((1, Q_RANK)),
            full((Q_RANK, 2 * N_HEADS * QK_NOPE)), full((N_HEADS, QK_NOPE, KV_RANK)),
            full((1, KV_RANK)), full((CONV_K, CONV_WIDTH)),
            pl.BlockSpec((tm, LANES), lambda i, j: (j, 0)), prev, prev,
            full((1, CONV_WIDTH)),
        ],
        out_specs=out_specs,
        out_shape=out_shapes,
        scratch_shapes=[pltpu.VMEM((tm + 2 * SUBLANES, CONV_WIDTH), jnp.float32)],
        compiler_params=_cparams("arbitrary", "arbitrary"),
        name="proj_seq" if seq_conv else "proj_tok",
    )(x, wts["g_mix"], wts["w_in"], wts["g_q"], wts["w_q"], wts["w_uk"], wts["g_kv"],
      wts["conv_w"], cs, s0, s1, wts["g_conv"])


def _attn_out(o, wuv_ref, gattn_ref, hd):
    a = jnp.dot(_bf(o), wuv_ref[hd], preferred_element_type=jnp.float32)
    return _bf(_rms(a, gattn_ref[:, hd * V_DIM:(hd + 1) * V_DIM]))


SOFTMAX_ROWS = 128


def _flash_kernel(tq, tk, q_ref, k_ref, wuv_ref, gattn_ref, out_ref, m_ref, l_ref, acc_ref, s_ref, p_ref):
    qi = pl.program_id(1)
    ki = pl.program_id(2)
    rows = N_HEADS * tq
    last_k = (qi * tq + tq - 1) // tk

    @pl.when(ki == 0)
    def _():
        m_ref[...] = jnp.full(m_ref.shape, NEG_INF, jnp.float32)
        l_ref[...] = jnp.zeros(l_ref.shape, jnp.float32)
        acc_ref[...] = jnp.zeros(acc_ref.shape, jnp.float32)

    first_masked = (qi * tq + 1) // tk

    def step(masked):
        k = k_ref[0]
        s_ref[...] = lax.dot_general(q_ref[0].reshape(rows, QK_PAD), k, _NT,
                                     preferred_element_type=jnp.float32)

        for c in range(rows // SOFTMAX_ROWS):
            rs = pl.ds(c * SOFTMAX_ROWS, SOFTMAX_ROWS)
            s = s_ref[rs, :]
            if masked:
                t0 = (c % (tq // SOFTMAX_ROWS)) * SOFTMAX_ROWS
                qpos = qi * tq + t0 + lax.broadcasted_iota(jnp.int32, (SOFTMAX_ROWS, tk), 0)
                kpos = ki * tk + lax.broadcasted_iota(jnp.int32, (SOFTMAX_ROWS, tk), 1)
                s = jnp.where(kpos <= qpos, s, NEG_INF)
            m_prev = m_ref[rs, :]
            m_new = jnp.maximum(m_prev, jnp.max(s, axis=-1, keepdims=True))
            alpha = jnp.exp(m_prev - m_new)
            p = jnp.exp(s - jnp.concatenate([m_new] * (tk // LANES), axis=1))
            l_ref[rs, :] = alpha * l_ref[rs, :] + jnp.sum(p, axis=-1, keepdims=True)
            acc_ref[rs, :] = acc_ref[rs, :] * jnp.concatenate([alpha] * (KV_RANK // LANES), axis=1)
            p_ref[rs, :] = _bf(p)
            m_ref[rs, :] = m_new
        acc_ref[...] += jnp.dot(p_ref[...], k[:, 0:KV_RANK], preferred_element_type=jnp.float32)

    pl.when(ki < first_masked)(functools.partial(step, False))
    pl.when((ki >= first_masked) & (ki <= last_k))(functools.partial(step, True))

    @pl.when(ki == last_k)
    def _():
        o = acc_ref[...] / l_ref[:, 0:1]
        for hd in range(N_HEADS):
            out_ref[0, :, hd * V_DIM:(hd + 1) * V_DIM] = _attn_out(
                o[hd * tq:(hd + 1) * tq], wuv_ref, gattn_ref, hd)


def _flash(qcat, kcat, wts):
    b, _, s, _ = qcat.shape
    tq, tk = 256, 512
    rows = N_HEADS * tq
    return pl.pallas_call(
        functools.partial(_flash_kernel, tq, tk),
        grid=(b, s // tq, s // tk),
        in_specs=[
            pl.BlockSpec((1, N_HEADS, tq, QK_PAD), lambda i, q, k: (i, 0, q, 0)),
            pl.BlockSpec((1, tk, QK_PAD),
                         lambda i, q, k: (i, jnp.minimum(k, (q * tq + tq - 1) // tk), 0)),
            pl.BlockSpec((N_HEADS, KV_RANK, V_DIM), lambda i, q, k: (0, 0, 0)),
            pl.BlockSpec((1, ATTN_WIDTH), lambda i, q, k: (0, 0)),
        ],
        out_specs=pl.BlockSpec((1, tq, ATTN_WIDTH), lambda i, q, k: (i, q, 0)),
        out_shape=jax.ShapeDtypeStruct((b, s, ATTN_WIDTH), jnp.bfloat16),
        scratch_shapes=[pltpu.VMEM((rows, LANES), jnp.float32),
                        pltpu.VMEM((rows, LANES), jnp.float32),
                        pltpu.VMEM((rows, KV_RANK), jnp.float32),
                        pltpu.VMEM((rows, tk), jnp.float32),
                        pltpu.VMEM((rows, tk), jnp.bfloat16)],
        compiler_params=_cparams("arbitrary", "arbitrary", "arbitrary"),
        name="flash",
    )(qcat, kcat, wts["w_uv"], wts["g_attn"])


PAGES_PER_STEP = 16
Q_ROWS = 16


def _decode_kernel(n_steps, pt_ref, q_ref, knew_ref, *refs):
    ckv_refs = refs[0:PAGES_PER_STEP]
    kr_refs = refs[PAGES_PER_STEP:2 * PAGES_PER_STEP]
    out_ref, m_ref, l_ref, acc_ref = refs[2 * PAGES_PER_STEP:]
    j = pl.program_id(1)

    @pl.when(j == 0)
    def _():
        m_ref[...] = jnp.full(m_ref.shape, NEG_INF, jnp.float32)
        l_ref[...] = jnp.zeros(l_ref.shape, jnp.float32)
        acc_ref[...] = jnp.zeros(acc_ref.shape, jnp.float32)

    q = q_ref[0]
    qlat = q[:, 0:KV_RANK]
    qrope = q[:, KV_RANK:KV_RANK + QK_ROPE]
    vals = [_bf(r[0, 0]) for r in ckv_refs]
    s = jnp.concatenate(
        [lax.dot_general(qlat, v, _NT, preferred_element_type=jnp.float32)
         + jnp.dot(qrope, _bf(r[0, 0]), preferred_element_type=jnp.float32)
         for v, r in zip(vals, kr_refs)], axis=-1)
    m_prev = m_ref[...]
    m_new = jnp.maximum(m_prev, jnp.max(s, axis=-1, keepdims=True))
    alpha = jnp.exp(m_prev - m_new)
    p = jnp.exp(s - m_new[:, 0:1])
    l_new = alpha * l_ref[...] + jnp.sum(p, axis=-1, keepdims=True)
    acc = acc_ref[...] * alpha[:, 0:1]
    for i, v in enumerate(vals):
        acc = acc + jnp.dot(_bf(p[:, i * PAGE_SIZE:(i + 1) * PAGE_SIZE]), v,
                            preferred_element_type=jnp.float32)
    m_ref[...] = m_new
    l_ref[...] = l_new
    acc_ref[...] = acc

    @pl.when(j == n_steps - 1)
    def _():
        kn = knew_ref[0].astype(jnp.float32)
        s_new = jnp.sum(q.astype(jnp.float32) * kn, axis=-1, keepdims=True)
        m_fin = jnp.maximum(m_new, s_new)
        a2 = jnp.exp(m_new - m_fin)
        p_new = jnp.exp(s_new - m_fin[:, 0:1])
        l_fin = a2 * l_new + p_new
        acc_fin = acc * a2[:, 0:1] + _bf(p_new).astype(jnp.float32) * kn[:, 0:KV_RANK]
        out_ref[0] = acc_fin / l_fin[:, 0:1]


def _decode(page_table, q, knew, cache_ckv, cache_krope_t):
    nb, n_pages = page_table.shape
    n_steps = n_pages // PAGES_PER_STEP

    def page_spec(rows, width, i):
        return pl.BlockSpec((1, 1, rows, width),
                            lambda b, j, pt: (0, pt[b, j * PAGES_PER_STEP + i], 0, 0))

    grid_spec = pltpu.PrefetchScalarGridSpec(
        num_scalar_prefetch=1,
        grid=(nb, n_steps),
        in_specs=[pl.BlockSpec((1, Q_ROWS, QK_PAD), lambda b, j, pt: (b, 0, 0)),
                  pl.BlockSpec((1, 1, QK_PAD), lambda b, j, pt: (b, 0, 0))]
        + [page_spec(PAGE_SIZE, KV_RANK, i) for i in range(PAGES_PER_STEP)]
        + [page_spec(QK_ROPE, PAGE_SIZE, i) for i in range(PAGES_PER_STEP)],
        out_specs=pl.BlockSpec((1, Q_ROWS, KV_RANK), lambda b, j, pt: (b, 0, 0)),
        scratch_shapes=[pltpu.VMEM((Q_ROWS, LANES), jnp.float32),
                        pltpu.VMEM((Q_ROWS, LANES), jnp.float32),
                        pltpu.VMEM((Q_ROWS, KV_RANK), jnp.float32)],
    )
    return pl.pallas_call(
        functools.partial(_decode_kernel, n_steps),
        grid_spec=grid_spec,
        out_shape=jax.ShapeDtypeStruct((nb, Q_ROWS, KV_RANK), jnp.float32),
        compiler_params=_cparams("arbitrary", "arbitrary"),
        name="decode",
    )(page_table, q, knew, *([cache_ckv] * PAGES_PER_STEP), *([cache_krope_t] * PAGES_PER_STEP))


def _attn_out_kernel(o_ref, wuv_ref, gattn_ref, out_ref):
    for hd in range(N_HEADS):
        out_ref[:, hd * V_DIM:(hd + 1) * V_DIM] = _attn_out(o_ref[hd], wuv_ref, gattn_ref, hd)


def _attn_out_call(o, wts):
    t = o.shape[1]
    return pl.pallas_call(
        _attn_out_kernel,
        out_shape=jax.ShapeDtypeStruct((t, ATTN_WIDTH), jnp.bfloat16),
        name="attn_out",
    )(o, wts["w_uv"], wts["g_attn"])


def _mix_kernel(attn_ref, conv_ref, x_ref, wo_ref, gffn_ref, wqry_ref, keys_ref,
                xp_ref, xn_ref, sc_ref):
    y = (jnp.dot(attn_ref[...], wo_ref[0:ATTN_WIDTH, :], preferred_element_type=jnp.float32)
         + jnp.dot(conv_ref[...], wo_ref[ATTN_WIDTH:, :], preferred_element_type=jnp.float32))
    xp = x_ref[...] + y
    xp_ref[...] = xp
    xn = _rms(xp, gffn_ref[...])
    xn_ref[...] = xn
    q = jnp.dot(_bf(xn), wqry_ref[...], preferred_element_type=jnp.float32)
    half = PEER_DK // 2
    for g in range(GROUPS):
        sc_ref[g] = lax.dot_general(keys_ref[g % 2], _bf(q[:, g * half:(g + 1) * half]), _NT,
                                    preferred_element_type=jnp.float32)


def _mix(attn, conv, x, wts):
    t = x.shape[0]
    tm = min(256, t)
    full = lambda shape: pl.BlockSpec(shape, lambda i: (0,) * len(shape))
    tok = lambda width: pl.BlockSpec((tm, width), lambda i: (i, 0))
    return pl.pallas_call(
        _mix_kernel,
        grid=(t // tm,),
        in_specs=[tok(ATTN_WIDTH), tok(CONV_WIDTH), tok(D_MODEL), full((D_MODEL, D_MODEL)),
                  full((1, D_MODEL)), full((D_MODEL, PEER_HEADS * PEER_DK)),
                  full((2, PEER_KEYS, PEER_DK // 2))],
        out_specs=(tok(D_MODEL), tok(D_MODEL),
                   pl.BlockSpec((GROUPS, PEER_KEYS, tm), lambda i: (0, 0, i))),
        out_shape=(jax.ShapeDtypeStruct((t, D_MODEL), jnp.float32),
                   jax.ShapeDtypeStruct((t, D_MODEL), jnp.float32),
                   jax.ShapeDtypeStruct((GROUPS, PEER_KEYS, t), jnp.float32)),
        compiler_params=_cparams("arbitrary"),
        name="mix",
    )(attn, conv, x, wts["w_o"], wts["g_ffn"], wts["w_query"], wts["sub_keys"])


def _top_rounds(s, ids, payload, n):
    vals, idxs, pays = [], [], []
    sentinel = jnp.int32(2 ** 30)
    for _ in range(n):
        m = jnp.max(s, axis=0, keepdims=True)
        win = jnp.min(jnp.where(s == m, ids, sentinel), axis=0, keepdims=True)
        hit = ids == win
        vals.append(m)
        idxs.append(win)
        if payload is not None:
            pays.append(jnp.max(jnp.where(hit, payload, -1), axis=0, keepdims=True))
        s = jnp.where(hit, NEG_INF, s)
    cat = lambda xs: jnp.concatenate(xs, axis=0)
    return cat(vals), cat(idxs), (cat(pays) if payload is not None else None)


def _topk_kernel(tt, sc_ref, idx_ref, gate_ref):
    k = PEER_TOPK
    key_ids = lax.broadcasted_iota(jnp.int32, (PEER_KEYS, tt), 0)
    blocks = [(a, a + 1, 0, k) for a in range(2)] + [(a, a + 1, 0, 8) for a in range(2, 8)] + [(8, k, 0, 1)]
    for hd in range(PEER_HEADS):
        v0, i0, _ = _top_rounds(sc_ref[2 * hd], key_ids, None, k)
        v1, i1, _ = _top_rounds(sc_ref[2 * hd + 1], key_ids, None, k)
        cand, cid, eid = [], [], []
        for (a0, a1, b0, b1) in blocks:
            na, nb = a1 - a0, b1 - b0
            if na == 1:
                va, ia, vb, ib = v0[a0:a1], i0[a0:a1], v1[b0:b1], i1[b0:b1]
                rank = a0 * k + b0 + lax.broadcasted_iota(jnp.int32, (nb, tt), 0)
            else:
                va, ia, vb, ib = v0[a0:a1], i0[a0:a1], v1[b0:b1], i1[b0:b1]
                rank = (a0 + lax.broadcasted_iota(jnp.int32, (na, tt), 0)) * k + b0
            cand.append(va + vb)
            eid.append(ia * PEER_KEYS + ib)
            cid.append(rank)
        cand = jnp.concatenate(cand, axis=0)
        cid = jnp.concatenate(cid, axis=0)
        eid = jnp.concatenate(eid, axis=0)
        ts, _, te = _top_rounds(cand, cid, eid, k)
        e = jnp.exp(ts - ts[0:1])
        g = e / jnp.sum(e, axis=0, keepdims=True)
        idx_ref[hd * k:(hd + 1) * k, :] = te * PACKED_ROWS
        gate_ref[hd * k:(hd + 1) * k, :] = g


def _topk(scores):
    t = scores.shape[2]
    tt = LANES
    return pl.pallas_call(
        functools.partial(_topk_kernel, tt),
        grid=(t // tt,),
        in_specs=[pl.BlockSpec((GROUPS, PEER_KEYS, tt), lambda i: (0, 0, i))],
        out_specs=(pl.BlockSpec((N_SEL, tt), lambda i: (0, i)),
                   pl.BlockSpec((N_SEL, tt), lambda i: (0, i))),
        out_shape=(jax.ShapeDtypeStruct((N_SEL, t), jnp.int32),
                   jax.ShapeDtypeStruct((N_SEL, t), jnp.float32)),
        compiler_params=_cparams("arbitrary"),
        name="topk",
    )(scores)


TOK_BLOCK = 128
TOK_GROUP = 8
N_SLOTS = 4
PACKED_ROWS = ROWS_PER_EXPERT // 2
PACK_BLOCK = 8192


def _split_bf16(x):
    hi = _bf(x)
    lo = _bf(x - hi.astype(jnp.float32))
    return hi, lo


def _pack_kernel(x_ref, o_ref):
    o_ref[...] = pltpu.bitcast(_bf(x_ref[...]), jnp.int32)


def _pack_table(tab):
    rows = tab.size // LANES
    return pl.pallas_call(
        _pack_kernel,
        grid=(rows // PACK_BLOCK,),
        in_specs=[pl.BlockSpec((PACK_BLOCK, LANES), lambda i: (i, 0))],
        out_specs=pl.BlockSpec((PACK_BLOCK // 2, LANES), lambda i: (i, 0)),
        out_shape=jax.ShapeDtypeStruct((rows // 2, LANES), jnp.int32),
        compiler_params=_cparams("arbitrary"),
        name="pack_table",
    )(tab.reshape(rows, LANES))


def _gather_tile(tab_ref, off):
    w = tab_ref[pl.ds(pl.multiple_of(off, PACKED_ROWS), PACKED_ROWS), :]
    return pltpu.bitcast(w, jnp.bfloat16).astype(jnp.float32)


def _token_tile(xg, tl):
    return jnp.concatenate(
        [xg[tl:tl + 1, r * LANES:(r + 1) * LANES] for r in range(ROWS_PER_EXPERT)], axis=0)


def _staged_groups(idx_ref, slots, sem, process):
    n_slots = len(slots)
    n_groups = idx_ref.shape[0] // TOK_GROUP
    assert n_slots % 2 == 0 and n_groups % n_slots == 0

    def copy(g, s):
        rows = pl.ds(pl.multiple_of(g * TOK_GROUP, TOK_GROUP), TOK_GROUP)
        return pltpu.make_async_copy(idx_ref.at[rows, :], slots[s], sem.at[s])

    for s in range(n_slots - 1):
        copy(s, s).start()

    def ring(i, carry):
        for s in range(n_slots):
            g = n_slots * i + s
            copy(g, s).wait()
            ahead = n_slots - 1

            @pl.when(g + ahead < n_groups)
            def _():
                copy(g + ahead, (s + ahead) % n_slots).start()

            process(g, s % 2, slots[s])
        return carry

    lax.fori_loop(0, n_groups // n_slots, ring, 0)
    return n_groups


FOLD_BLOCK = SUBLANES
P_PITCH = ROWS_PER_EXPERT + 1


def _peer_u_kernel(idx_ref, x_ref, gate_ref, tab_ref, out_ref, *scratch):
    slots, sem = scratch[:N_SLOTS], scratch[N_SLOTS]
    f0_ref, f1_ref = scratch[N_SLOTS + 1:N_SLOTS + 3]
    p_refs = scratch[N_SLOTS + 3:]
    ones = jnp.ones((SUBLANES, LANES), jnp.bfloat16)
    sub = lax.broadcasted_iota(jnp.int32, (SUBLANES, LANES), 0)
    f_refs = (f0_ref, f1_ref)

    def lane_fold(f_ref, tl, rows):
        f_hi, f_lo = _split_bf16(f_ref[tl * N_SEL:(tl + 1) * N_SEL, :])
        a = (lax.dot_general(ones, f_hi, _NT, preferred_element_type=jnp.float32)
             + lax.dot_general(ones, f_lo, _NT, preferred_element_type=jnp.float32))
        return a if rows is None else jnp.where(sub == tl, a, rows)

    def store_rows(group, rows):
        out_ref[pl.ds(pl.multiple_of(group * TOK_GROUP, TOK_GROUP), TOK_GROUP), :] = rows

    def process(g, s, slot):
        xg = x_ref[pl.ds(pl.multiple_of(g * TOK_GROUP, TOK_GROUP), TOK_GROUP), :]
        rows = None
        for tl in range(TOK_GROUP + 1):
            if tl < TOK_GROUP:
                xt = _token_tile(xg, tl)
                rows = lane_fold(f_refs[1 - s], tl, rows)
            for kb in range(N_SEL // FOLD_BLOCK):
                if tl < TOK_GROUP:
                    for k in range(kb * FOLD_BLOCK, (kb + 1) * FOLD_BLOCK):
                        p_refs[tl][k * P_PITCH:k * P_PITCH + ROWS_PER_EXPERT, :] = (
                            _gather_tile(tab_ref, slot[tl, k]) * xt)
                if tl >= 1:
                    p_ref = p_refs[tl - 1]
                    first = kb * FOLD_BLOCK * P_PITCH
                    f = p_ref[pl.ds(first, FOLD_BLOCK, stride=P_PITCH), :]
                    for r in range(1, ROWS_PER_EXPERT):
                        f = f + p_ref[pl.ds(first + r, FOLD_BLOCK, stride=P_PITCH), :]
                    row = (tl - 1) * N_SEL + kb * FOLD_BLOCK
                    f_refs[s][row:row + FOLD_BLOCK, :] = f
        store_rows(jnp.maximum(g - 1, 0), rows)

    f1_ref[...] = jnp.zeros(f1_ref.shape, jnp.float32)
    n_groups = _staged_groups(idx_ref, slots, sem, process)
    rows = None
    for tl in range(TOK_GROUP):
        rows = lane_fold(f_refs[(n_groups - 1) % 2], tl, rows)
    store_rows(n_groups - 1, rows)
    a = out_ref[...]
    out_ref[...] = gate_ref[...] * (0.5 * a * (1.0 + lax.erf(a * (2.0 ** -0.5))))


def _stage_scratch():
    return [pltpu.SMEM((TOK_GROUP, N_SEL), jnp.int32)] * N_SLOTS + [pltpu.SemaphoreType.DMA((N_SLOTS,))]


def _peer_u(idx, xn, gates, tab):
    t = idx.shape[0]
    tb = min(TOK_BLOCK, t)
    sel = pl.BlockSpec((tb, N_SEL), lambda i: (i, 0))
    return pl.pallas_call(
        _peer_u_kernel,
        grid=(t // tb,),
        in_specs=[sel, pl.BlockSpec((tb, D_MODEL), lambda i: (i, 0)), sel,
                  pl.BlockSpec(memory_space=pltpu.VMEM)],
        out_specs=sel,
        out_shape=jax.ShapeDtypeStruct((t, N_SEL), jnp.float32),
        scratch_shapes=_stage_scratch()
        + [pltpu.VMEM((TOK_GROUP * N_SEL, LANES), jnp.float32)] * 2
        + [pltpu.VMEM((N_SEL * P_PITCH, LANES), jnp.float32)] * TOK_GROUP,
        compiler_params=_cparams("arbitrary"),
        name="peer_u",
    )(idx, xn, gates, tab)


def _peer_v_kernel(idx_ref, w_ref, tab_ref, out_ref, *scratch):
    slots, sem = scratch[:N_SLOTS], scratch[N_SLOTS]
    wb0_ref, wb1_ref, acc_ref = scratch[N_SLOTS + 1:]
    ones = jnp.ones((LANES, LANES), jnp.bfloat16)
    eye = (lax.broadcasted_iota(jnp.int32, (N_SEL, LANES), 0)
           == lax.broadcasted_iota(jnp.int32, (N_SEL, LANES), 1)).astype(jnp.float32)
    wb_refs = (wb0_ref, wb1_ref)
    last_group = idx_ref.shape[0] // TOK_GROUP - 1

    def group_weights(group):
        return w_ref[pl.ds(pl.multiple_of(group * TOK_GROUP, TOK_GROUP), TOK_GROUP), :]

    def broadcast_weights(wb_ref, wg, tl):
        d_hi, d_lo = _split_bf16(eye * wg[tl:tl + 1, :])
        wb_ref[tl * N_SEL:(tl + 1) * N_SEL, :] = (
            jnp.dot(d_hi, ones, preferred_element_type=jnp.float32)
            + jnp.dot(d_lo, ones, preferred_element_type=jnp.float32))

    def process(g, s, slot):
        wg_next = group_weights(jnp.minimum(g + 1, last_group))
        wb_ref = wb_refs[s]
        for tl in range(TOK_GROUP):
            broadcast_weights(wb_refs[1 - s], wg_next, tl)
            acc = acc_ref.at[tl * ROWS_PER_EXPERT:(tl + 1) * ROWS_PER_EXPERT, :]
            for kb in range(N_SEL // FOLD_BLOCK):
                part = None
                for k in range(kb * FOLD_BLOCK, (kb + 1) * FOLD_BLOCK):
                    row = tl * N_SEL + k
                    w = jnp.broadcast_to(wb_ref[row:row + 1, :], (ROWS_PER_EXPERT, LANES))
                    term = w * _gather_tile(tab_ref, slot[tl, k])
                    part = term if part is None else part + term
                acc[...] = part if kb == 0 else acc[...] + part
        tiles = [acc_ref[tl * ROWS_PER_EXPERT:(tl + 1) * ROWS_PER_EXPERT, :] for tl in range(TOK_GROUP)]
        base = pl.multiple_of(g * TOK_GROUP, TOK_GROUP)
        for r in range(ROWS_PER_EXPERT):
            out_ref[pl.ds(base, TOK_GROUP), r * LANES:(r + 1) * LANES] = jnp.concatenate(
                [tile[r:r + 1, :] for tile in tiles], axis=0)

    wg0 = group_weights(0)
    for tl in range(TOK_GROUP):
        broadcast_weights(wb0_ref, wg0, tl)
    _staged_groups(idx_ref, slots, sem, process)


def _peer_v(idx, w, tab):
    t = idx.shape[0]
    tb = min(TOK_BLOCK, t)
    sel = pl.BlockSpec((tb, N_SEL), lambda i: (i, 0))
    return pl.pallas_call(
        _peer_v_kernel,
        grid=(t // tb,),
        in_specs=[sel, sel, pl.BlockSpec(memory_space=pltpu.VMEM)],
        out_specs=pl.BlockSpec((tb, D_MODEL), lambda i: (i, 0)),
        out_shape=jax.ShapeDtypeStruct((t, D_MODEL), jnp.float32),
        scratch_shapes=_stage_scratch() + [pltpu.VMEM((TOK_GROUP * N_SEL, LANES), jnp.float32)] * 2
        + [pltpu.VMEM((TOK_GROUP * ROWS_PER_EXPERT, LANES), jnp.float32)],
        compiler_params=_cparams("arbitrary"),
        name="peer_v",
    )(idx, w, tab)


def _final_kernel(xp_ref, o_ref, g_ref, y_ref):
    y_ref[...] = _rms(xp_ref[...] + o_ref[...], g_ref[...])


def _final(xp, o, g):
    t = xp.shape[0]
    tm = min(512, t)
    tok = pl.BlockSpec((tm, D_MODEL), lambda i: (i, 0))
    return pl.pallas_call(
        _final_kernel,
        grid=(t // tm,),
        in_specs=[tok, tok, pl.BlockSpec((1, D_MODEL), lambda i: (0, 0))],
        out_specs=tok,
        out_shape=jax.ShapeDtypeStruct((t, D_MODEL), jnp.float32),
        compiler_params=_cparams("arbitrary"),
        name="final",
    )(xp, o, g)


def _rope_table(pos):
    inv = ROPE_THETA ** (-jnp.arange(0, QK_ROPE, 2, dtype=jnp.float32) / QK_ROPE)
    ang = pos.astype(jnp.float32)[:, None] * inv[None, :]
    c, s = jnp.cos(ang), jnp.sin(ang)
    return jnp.concatenate([c, c, -s, s], axis=-1)


def _swap_halves(w):
    half = w.shape[-1] // 2
    return jnp.concatenate([w[..., half:], w[..., :half]], axis=-1)


def _prepare_weights(g_mix_norm, w_in, g_q, w_uq, g_kv, w_uk, w_uv, conv_w, g_attn_out, g_conv_out,
                     w_o, g_ffn_norm, w_query, sub_keys, expert_u, expert_v):
    o1 = Q_RANK
    o2 = o1 + KV_RANK
    o3 = o2 + QK_ROPE
    w_kr = w_in[:, o2:o3]
    w_inx = jnp.concatenate([w_in[:, :o3], _swap_halves(w_kr), w_in[:, o3:]], axis=1)
    q_nope = w_uq[:, :, :QK_NOPE].reshape(Q_RANK, N_HEADS * QK_NOPE)
    q_rope = w_uq[:, :, QK_NOPE:]
    q_pair = jnp.concatenate([q_rope, _swap_halves(q_rope)], axis=-1).reshape(Q_RANK, N_HEADS * LANES)
    row = lambda g: g.reshape(1, -1)
    return {
        "g_mix": row(g_mix_norm), "w_in": _bf(w_inx), "g_q": row(g_q),
        "w_q": _bf(jnp.concatenate([q_nope, q_pair], axis=1)),
        "w_uk": _bf(jnp.transpose(w_uk, (1, 2, 0))), "g_kv": row(g_kv),
        "w_uv": _bf(jnp.transpose(w_uv, (1, 0, 2))), "conv_w": conv_w,
        "g_attn": row(g_attn_out), "g_conv": row(g_conv_out), "w_o": _bf(w_o),
        "g_ffn": row(g_ffn_norm), "w_query": _bf(w_query), "sub_keys": _bf(sub_keys),
        "tab_u": _pack_table(expert_u), "tab_v": _pack_table(expert_v),
    }


def _channel_mixer(attn, conv, x, wts, g_final):
    t = x.shape[0]
    xp, xn, scores = _mix(attn, conv, x, wts)
    idx_t, gate_t = _topk(scores)
    idx, gates = idx_t.T, gate_t.T
    w = _peer_u(idx, xn, gates, wts["tab_u"])
    o = _peer_v(idx, w, wts["tab_v"])
    return _final(xp, o, g_final.reshape(1, D_MODEL))


def kernel(x_prompt, x_sample, cache_ckv, cache_krope, state_conv, page_table, g_mix_norm, w_in, g_q, w_uq, g_kv, w_uk, w_uv, conv_w, g_attn_out, g_conv_out, w_o, g_ffn_norm, w_query, sub_keys, expert_u, expert_v, g_final):
    assert w_in.shape[0] == 1, "single-layer step"
    wts = _prepare_weights(g_mix_norm[0], w_in[0], g_q[0], w_uq[0], g_kv[0], w_uk[0], w_uv[0], conv_w[0],
                           g_attn_out[0], g_conv_out[0], w_o[0], g_ffn_norm[0], w_query[0], sub_keys[0],
                           expert_u[0], expert_v[0])
    b, s, _ = x_prompt.shape
    nb, n_pages = page_table.shape
    past = n_pages * PAGE_SIZE

    zero_prev = jnp.zeros((SUBLANES, CONV_WIDTH), jnp.float32)
    qcat, kcat, ckv_p, kr_p, conv_p, tail_p = _proj(
        x_prompt, _rope_table(jnp.arange(s)), zero_prev, zero_prev, wts, True)
    attn_p = _flash(qcat, kcat, wts)
    y_prompt = _channel_mixer(attn_p.reshape(b * s, ATTN_WIDTH), conv_p.reshape(b * s, CONV_WIDTH),
                              x_prompt.reshape(b * s, D_MODEL), wts, g_final).reshape(b, s, D_MODEL)
    new_conv_p = tail_p[:, SUBLANES - (CONV_K - 1):, :]

    xs = x_sample.reshape(1, nb, D_MODEL)
    cs_s = jnp.broadcast_to(_rope_table(jnp.full((1,), past)), (nb, LANES))
    s0, s1 = state_conv[0, :, 0, :], state_conv[0, :, 1, :]
    qcat_s, kcat_s, ckv_s, kr_s, conv_s, z_s = _proj(xs, cs_s, s0, s1, wts, False)
    q_dec = jnp.pad(jnp.transpose(qcat_s[0], (1, 0, 2)), ((0, 0), (0, Q_ROWS - N_HEADS), (0, 0)))
    o_lat = _decode(page_table, q_dec, kcat_s.reshape(nb, 1, QK_PAD), cache_ckv,
                    jnp.swapaxes(cache_krope, 2, 3))
    attn_s = _attn_out_call(jnp.transpose(o_lat[:, :N_HEADS], (1, 0, 2)), wts)
    y_sample = _channel_mixer(attn_s, conv_s[0], x_sample.reshape(nb, D_MODEL), wts, g_final)
    new_conv_s = jnp.stack([s1, z_s[0]], axis=1)

    return (y_prompt, y_sample.reshape(nb, 1, D_MODEL),
            ckv_p[None], kr_p[None], new_conv_p[None],
            ckv_s.reshape(1, nb, 1, KV_RANK), kr_s.reshape(1, nb, 1, QK_ROPE), new_conv_s[None])
```

```python
import functools

import jax
import jax.numpy as jnp
import numpy as np
from jax import lax
from jax.experimental import pallas as pl
from jax.experimental.pallas import tpu as pltpu

D_MODEL = 1024
N_HEADS = 4
QK_NOPE = 128
QK_ROPE = 64
V_DIM = 128
Q_RANK = 384
KV_RANK = 256
ATTN_WIDTH = N_HEADS * V_DIM
CONV_WIDTH = D_MODEL - ATTN_WIDTH
CONV_GROUPS = 4
CONV_K = 3
PEER_HEADS = 8
PEER_KEYS = 128
PEER_DK = 256
PEER_TOPK = 16
PAGE_SIZE = 128
ROPE_THETA = 10000.0
EPS = 1e-6
SM_SCALE = (QK_NOPE + QK_ROPE) ** -0.5

LANES = 128
SUBLANES = 8
VMEM_LIMIT = 48 * 1024 * 1024

QK_PAD = 384
N_SEL = PEER_HEADS * PEER_TOPK
ROWS_PER_EXPERT = D_MODEL // LANES
GROUPS = 2 * PEER_HEADS
D_INX = Q_RANK + KV_RANK + 2 * QK_ROPE + 3 * CONV_WIDTH
NEG_INF = float("-inf")

_NT = (((1,), (1,)), ((), ()))


def _cparams(*sem):
    return pltpu.CompilerParams(dimension_semantics=sem, vmem_limit_bytes=VMEM_LIMIT)


def _rms(x, g):
    return x * lax.rsqrt(jnp.mean(x * x, axis=-1, keepdims=True) + EPS) * g


def _bf(x):
    return x.astype(jnp.bfloat16)


def _rope_pair(t, cs):
    r = t * cs
    return r + pltpu.roll(r, QK_ROPE, axis=1)


def _proj_kernel(seq_conv, tm, x_ref, gmix_ref, win_ref, gq_ref, wq_ref, wuk_ref, gkv_ref,
                 convw_ref, cs_ref, s0_ref, s1_ref, gconv_ref,
                 qcat_ref, kcat_ref, ckv_ref, kr_ref, convn_ref, z_ref, zbuf_ref):
    si = pl.program_id(1)
    x = x_ref[0]
    xn = _bf(_rms(x, gmix_ref[...]))
    h = jnp.dot(xn, win_ref[...], preferred_element_type=jnp.float32)
    o1 = Q_RANK
    o2 = o1 + KV_RANK
    o3 = o2 + 2 * QK_ROPE
    o4 = o3 + CONV_WIDTH
    o5 = o4 + CONV_WIDTH
    cs = cs_ref[...]
    lane = lax.broadcasted_iota(jnp.int32, (tm, LANES), 1)
    rope_mask = lane < QK_ROPE

    ckv = _rms(h[:, o1:o2], gkv_ref[...])
    kr = _rope_pair(h[:, o2:o3], cs)
    ckv_ref[0] = ckv
    kr_ref[0] = kr[:, :QK_ROPE]
    kcat_ref[0, :, 0:KV_RANK] = _bf(ckv)
    kcat_ref[0, :, KV_RANK:QK_PAD] = _bf(jnp.where(rope_mask, kr, 0.0))

    cq = _bf(_rms(h[:, 0:o1], gq_ref[...]))
    q = jnp.dot(cq, wq_ref[...], preferred_element_type=jnp.float32)
    for hd in range(N_HEADS):
        qn = _bf(q[:, hd * QK_NOPE:(hd + 1) * QK_NOPE])
        qlat = jnp.dot(qn, wuk_ref[hd], preferred_element_type=jnp.float32)
        base = N_HEADS * QK_NOPE + hd * LANES
        qr = _rope_pair(q[:, base:base + LANES], cs)
        qcat_ref[0, hd, :, 0:KV_RANK] = _bf(qlat * SM_SCALE)
        qcat_ref[0, hd, :, KV_RANK:QK_PAD] = _bf(jnp.where(rope_mask, qr * SM_SCALE, 0.0))

    z = h[:, o4:o5] * h[:, o3:o4]
    zb = h[:, o5:]
    w0 = convw_ref[0:1, :]
    w1 = convw_ref[1:2, :]
    w2 = convw_ref[2:3, :]
    if seq_conv:
        @pl.when(si == 0)
        def _():
            zbuf_ref[0:SUBLANES, :] = jnp.zeros((SUBLANES, CONV_WIDTH), jnp.float32)

        zbuf_ref[SUBLANES:SUBLANES + tm, :] = z
        y = (w2 * z + w1 * zbuf_ref[SUBLANES - 1:SUBLANES - 1 + tm, :]
             + w0 * zbuf_ref[SUBLANES - 2:SUBLANES - 2 + tm, :])
        tail = zbuf_ref[tm:tm + SUBLANES, :]
        zbuf_ref[0:SUBLANES, :] = tail
        z_ref[0] = tail
    else:
        y = w2 * z + w1 * s1_ref[...] + w0 * s0_ref[...]
        z_ref[0] = z
    c = zb * y
    for g in range(CONV_GROUPS):
        sl = slice(g * LANES, (g + 1) * LANES)
        convn_ref[0, :, sl] = _bf(_rms(c[:, sl], gconv_ref[:, sl]))


def _proj(x, cs, s0, s1, wts, seq_conv):
    b, s, _ = x.shape
    tm = min(512, s)
    grid = (b, s // tm)
    z_rows = SUBLANES if seq_conv else tm
    full = lambda shape: pl.BlockSpec(shape, lambda i, j: (0,) * len(shape))
    tok = lambda width: pl.BlockSpec((1, tm, width), lambda i, j: (i, j, 0))
    prev = pl.BlockSpec((s0.shape[0] if seq_conv else tm, CONV_WIDTH),
                        lambda i, j: (0 if seq_conv else j, 0))
    out_shapes = (
        jax.ShapeDtypeStruct((b, N_HEADS, s, QK_PAD), jnp.bfloat16),
        jax.ShapeDtypeStruct((b, s, QK_PAD), jnp.bfloat16),
        jax.ShapeDtypeStruct((b, s, KV_RANK), jnp.float32),
        jax.ShapeDtypeStruct((b, s, QK_ROPE), jnp.float32),
        jax.ShapeDtypeStruct((b, s, CONV_WIDTH), jnp.bfloat16),
        jax.ShapeDtypeStruct((b, z_rows, CONV_WIDTH), jnp.float32),
    )
    out_specs = (
        pl.BlockSpec((1, N_HEADS, tm, QK_PAD), lambda i, j: (i, 0, j, 0)),
        tok(QK_PAD), tok(KV_RANK), tok(QK_ROPE), tok(CONV_WIDTH),
        pl.BlockSpec((1, z_rows, CONV_WIDTH), lambda i, j: (i, 0, 0)),
    )
    return pl.pallas_call(
        functools.partial(_proj_kernel, seq_conv, tm),
        grid=grid,
        in_specs=[
            tok(D_MODEL), full((1, D_MODEL)), full((D_MODEL, D_INX)), full((1, Q_RANK)),
            full((Q_RANK, 2 * N_HEADS * QK_NOPE)), full((N_HEADS, QK_NOPE, KV_RANK)),
            full((1, KV_RANK)), full((CONV_K, CONV_WIDTH)),
            pl.BlockSpec((tm, LANES), lambda i, j: (j, 0)), prev, prev,
            full((1, CONV_WIDTH)),
        ],
        out_specs=out_specs,
        out_shape=out_shapes,
        scratch_shapes=[pltpu.VMEM((tm + 2 * SUBLANES, CONV_WIDTH), jnp.float32)],
        compiler_params=_cparams("arbitrary", "arbitrary"),
        name="proj_seq" if seq_conv else "proj_tok",
    )(x, wts["g_mix"], wts["w_in"], wts["g_q"], wts["w_q"], wts["w_uk"], wts["g_kv"],
      wts["conv_w"], cs, s0, s1, wts["g_conv"])


def _attn_out(o, wuv_ref, gattn_ref, hd):
    a = jnp.dot(_bf(o), wuv_ref[hd], preferred_element_type=jnp.float32)
    return _bf(_rms(a, gattn_ref[:, hd * V_DIM:(hd + 1) * V_DIM]))


SOFTMAX_ROWS = 128


def _flash_kernel(tq, tk, q_ref, k_ref, wuv_ref, gattn_ref, out_ref, m_ref, l_ref, acc_ref, s_ref, p_ref):
    qi = pl.program_id(1)
    ki = pl.program_id(2)
    rows = N_HEADS * tq
    last_k = (qi * tq + tq - 1) // tk

    @pl.when(ki == 0)
    def _():
        m_ref[...] = jnp.full(m_ref.shape, NEG_INF, jnp.float32)
        l_ref[...] = jnp.zeros(l_ref.shape, jnp.float32)
        acc_ref[...] = jnp.zeros(acc_ref.shape, jnp.float32)

    first_masked = (qi * tq + 1) // tk

    def step(masked):
        k = k_ref[0]
        s_ref[...] = lax.dot_general(q_ref[0].reshape(rows, QK_PAD), k, _NT,
                                     preferred_element_type=jnp.float32)

        for c in range(rows // SOFTMAX_ROWS):
            rs = pl.ds(c * SOFTMAX_ROWS, SOFTMAX_ROWS)
            s = s_ref[rs, :]
            if masked:
                t0 = (c % (tq // SOFTMAX_ROWS)) * SOFTMAX_ROWS
                qpos = qi * tq + t0 + lax.broadcasted_iota(jnp.int32, (SOFTMAX_ROWS, tk), 0)
                kpos = ki * tk + lax.broadcasted_iota(jnp.int32, (SOFTMAX_ROWS, tk), 1)
                s = jnp.where(kpos <= qpos, s, NEG_INF)
            m_prev = m_ref[rs, :]
            m_new = jnp.maximum(m_prev, jnp.max(s, axis=-1, keepdims=True))
            alpha = jnp.exp(m_prev - m_new)
            p = jnp.exp(s - jnp.concatenate([m_new] * (tk // LANES), axis=1))
            l_ref[rs, :] = alpha * l_ref[rs, :] + jnp.sum(p, axis=-1, keepdims=True)
            acc_ref[rs, :] = acc_ref[rs, :] * jnp.concatenate([alpha] * (KV_RANK // LANES), axis=1)
            p_ref[rs, :] = _bf(p)
            m_ref[rs, :] = m_new
        acc_ref[...] += jnp.dot(p_ref[...], k[:, 0:KV_RANK], preferred_element_type=jnp.float32)

    pl.when(ki < first_masked)(functools.partial(step, False))
    pl.when((ki >= first_masked) & (ki <= last_k))(functools.partial(step, True))

    @pl.when(ki == last_k)
    def _():
        o = acc_ref[...] / l_ref[:, 0:1]
        for hd in range(N_HEADS):
            out_ref[0, :, hd * V_DIM:(hd + 1) * V_DIM] = _attn_out(
                o[hd * tq:(hd + 1) * tq], wuv_ref, gattn_ref, hd)


def _flash(qcat, kcat, wts):
    b, _, s, _ = qcat.shape
    tq, tk = 512, 512
    rows = N_HEADS * tq
    return pl.pallas_call(
        functools.partial(_flash_kernel, tq, tk),
        grid=(b, s // tq, s // tk),
        in_specs=[
            pl.BlockSpec((1, N_HEADS, tq, QK_PAD), lambda i, q, k: (i, 0, q, 0)),
            pl.BlockSpec((1, tk, QK_PAD),
                         lambda i, q, k: (i, jnp.minimum(k, (q * tq + tq - 1) // tk), 0)),
            pl.BlockSpec((N_HEADS, KV_RANK, V_DIM), lambda i, q, k: (0, 0, 0)),
            pl.BlockSpec((1, ATTN_WIDTH), lambda i, q, k: (0, 0)),
        ],
        out_specs=pl.BlockSpec((1, tq, ATTN_WIDTH), lambda i, q, k: (i, q, 0)),
        out_shape=jax.ShapeDtypeStruct((b, s, ATTN_WIDTH), jnp.bfloat16),
        scratch_shapes=[pltpu.VMEM((rows, LANES), jnp.float32),
                        pltpu.VMEM((rows, LANES), jnp.float32),
                        pltpu.VMEM((rows, KV_RANK), jnp.float32),
                        pltpu.VMEM((rows, tk), jnp.float32),
                        pltpu.VMEM((rows, tk), jnp.bfloat16)],
        compiler_params=_cparams("arbitrary", "arbitrary", "arbitrary"),
        name="flash",
    )(qcat, kcat, wts["w_uv"], wts["g_attn"])


PAGES_PER_STEP = 16
Q_ROWS = 16


def _decode_kernel(n_steps, pt_ref, q_ref, knew_ref, ckv_hbm, kr_hbm, out_ref,
                   ckv_buf, kr_buf, sem, m_ref, l_ref, acc_ref):
    b = pl.program_id(0)
    j = pl.program_id(1)
    total_steps = pl.num_programs(0) * n_steps
    step = b * n_steps + j
    slot = step % 2

    def page_copies(page_of, sl):
        copies = []
        for i in range(PAGES_PER_STEP):
            page = page_of(i)
            copies.append(pltpu.make_async_copy(ckv_hbm.at[0, page], ckv_buf.at[sl, i], sem.at[0, sl]))
            copies.append(pltpu.make_async_copy(kr_hbm.at[0, page], kr_buf.at[sl, i], sem.at[1, sl]))
        return copies

    @pl.when(step == 0)
    def _():
        for c in page_copies(lambda i: pt_ref[0, i], 0):
            c.start()

    nxt = step + 1

    @pl.when(nxt < total_steps)
    def _():
        nb_, nj_ = nxt // n_steps, nxt % n_steps
        for c in page_copies(lambda i: pt_ref[nb_, nj_ * PAGES_PER_STEP + i], 1 - slot):
            c.start()

    for c in page_copies(lambda i: 0, slot):
        c.wait()

    @pl.when(j == 0)
    def _():
        m_ref[...] = jnp.full(m_ref.shape, NEG_INF, jnp.float32)
        l_ref[...] = jnp.zeros(l_ref.shape, jnp.float32)
        acc_ref[...] = jnp.zeros(acc_ref.shape, jnp.float32)

    q = q_ref[0]
    qlat = q[:, 0:KV_RANK]
    qrope = q[:, KV_RANK:KV_RANK + QK_ROPE]
    vals = [_bf(ckv_buf[slot, i]) for i in range(PAGES_PER_STEP)]
    s = jnp.concatenate(
        [lax.dot_general(qlat, v, _NT, preferred_element_type=jnp.float32)
         + jnp.dot(qrope, _bf(kr_buf[slot, i]), preferred_element_type=jnp.float32)
         for i, v in enumerate(vals)], axis=-1)
    m_prev = m_ref[...]
    m_new = jnp.maximum(m_prev, jnp.max(s, axis=-1, keepdims=True))
    alpha = jnp.exp(m_prev - m_new)
    p = jnp.exp(s - m_new[:, 0:1])
    l_new = alpha * l_ref[...] + jnp.sum(p, axis=-1, keepdims=True)
    acc = acc_ref[...] * alpha[:, 0:1]
    for i, v in enumerate(vals):
        acc = acc + jnp.dot(_bf(p[:, i * PAGE_SIZE:(i + 1) * PAGE_SIZE]), v,
                            preferred_element_type=jnp.float32)
    m_ref[...] = m_new
    l_ref[...] = l_new
    acc_ref[...] = acc

    @pl.when(j == n_steps - 1)
    def _():
        kn = knew_ref[0].astype(jnp.float32)
        s_new = jnp.sum(q.astype(jnp.float32) * kn, axis=-1, keepdims=True)
        m_fin = jnp.maximum(m_new, s_new)
        a2 = jnp.exp(m_new - m_fin)
        p_new = jnp.exp(s_new - m_fin[:, 0:1])
        l_fin = a2 * l_new + p_new
        acc_fin = acc * a2[:, 0:1] + _bf(p_new).astype(jnp.float32) * kn[:, 0:KV_RANK]
        out_ref[0] = acc_fin / l_fin[:, 0:1]


def _decode(page_table, q, knew, cache_ckv, cache_krope_t):
    nb, n_pages = page_table.shape
    n_steps = n_pages // PAGES_PER_STEP

    grid_spec = pltpu.PrefetchScalarGridSpec(
        num_scalar_prefetch=1,
        grid=(nb, n_steps),
        in_specs=[pl.BlockSpec((1, Q_ROWS, QK_PAD), lambda b, j, pt: (b, 0, 0)),
                  pl.BlockSpec((1, 1, QK_PAD), lambda b, j, pt: (b, 0, 0)),
                  pl.BlockSpec(memory_space=pl.ANY), pl.BlockSpec(memory_space=pl.ANY)],
        out_specs=pl.BlockSpec((1, Q_ROWS, KV_RANK), lambda b, j, pt: (b, 0, 0)),
        scratch_shapes=[pltpu.VMEM((2, PAGES_PER_STEP, PAGE_SIZE, KV_RANK), jnp.float32),
                        pltpu.VMEM((2, PAGES_PER_STEP, QK_ROPE, PAGE_SIZE), jnp.float32),
                        pltpu.SemaphoreType.DMA((2, 2)),
                        pltpu.VMEM((Q_ROWS, LANES), jnp.float32),
                        pltpu.VMEM((Q_ROWS, LANES), jnp.float32),
                        pltpu.VMEM((Q_ROWS, KV_RANK), jnp.float32)],
    )
    return pl.pallas_call(
        functools.partial(_decode_kernel, n_steps),
        grid_spec=grid_spec,
        out_shape=jax.ShapeDtypeStruct((nb, Q_ROWS, KV_RANK), jnp.float32),
        compiler_params=_cparams("arbitrary", "arbitrary"),
        name="decode",
    )(page_table, q, knew, cache_ckv, cache_krope_t)


def _attn_out_kernel(o_ref, wuv_ref, gattn_ref, out_ref):
    for hd in range(N_HEADS):
        out_ref[:, hd * V_DIM:(hd + 1) * V_DIM] = _attn_out(o_ref[hd], wuv_ref, gattn_ref, hd)


def _attn_out_call(o, wts):
    t = o.shape[1]
    return pl.pallas_call(
        _attn_out_kernel,
        out_shape=jax.ShapeDtypeStruct((t, ATTN_WIDTH), jnp.bfloat16),
        name="attn_out",
    )(o, wts["w_uv"], wts["g_attn"])


def _mix_kernel(attn_ref, conv_ref, x_ref, wo_ref, gffn_ref, wqry_ref, keys_ref,
                xp_ref, xn_ref, sc_ref):
    y = (jnp.dot(attn_ref[...], wo_ref[0:ATTN_WIDTH, :], preferred_element_type=jnp.float32)
         + jnp.dot(conv_ref[...], wo_ref[ATTN_WIDTH:, :], preferred_element_type=jnp.float32))
    xp = x_ref[...] + y
    xp_ref[...] = xp
    xn = _rms(xp, gffn_ref[...])
    xn_ref[...] = xn
    q = jnp.dot(_bf(xn), wqry_ref[...], preferred_element_type=jnp.float32)
    half = PEER_DK // 2
    for g in range(GROUPS):
        sc_ref[g] = lax.dot_general(keys_ref[g % 2], _bf(q[:, g * half:(g + 1) * half]), _NT,
                                    preferred_element_type=jnp.float32)


def _mix(attn, conv, x, wts):
    t = x.shape[0]
    tm = min(256, t)
    full = lambda shape: pl.BlockSpec(shape, lambda i: (0,) * len(shape))
    tok = lambda width: pl.BlockSpec((tm, width), lambda i: (i, 0))
    return pl.pallas_call(
        _mix_kernel,
        grid=(t // tm,),
        in_specs=[tok(ATTN_WIDTH), tok(CONV_WIDTH), tok(D_MODEL), full((D_MODEL, D_MODEL)),
                  full((1, D_MODEL)), full((D_MODEL, PEER_HEADS * PEER_DK)),
                  full((2, PEER_KEYS, PEER_DK // 2))],
        out_specs=(tok(D_MODEL), tok(D_MODEL),
                   pl.BlockSpec((GROUPS, PEER_KEYS, tm), lambda i: (0, 0, i))),
        out_shape=(jax.ShapeDtypeStruct((t, D_MODEL), jnp.float32),
                   jax.ShapeDtypeStruct((t, D_MODEL), jnp.float32),
                   jax.ShapeDtypeStruct((GROUPS, PEER_KEYS, t), jnp.float32)),
        compiler_params=_cparams("arbitrary"),
        name="mix",
    )(attn, conv, x, wts["w_o"], wts["g_ffn"], wts["w_query"], wts["sub_keys"])


def _top_rounds(s, ids, payload, n):
    vals, idxs, pays = [], [], []
    sentinel = jnp.int32(2 ** 30)
    for _ in range(n):
        m = jnp.max(s, axis=0, keepdims=True)
        win = jnp.min(jnp.where(s == m, ids, sentinel), axis=0, keepdims=True)
        hit = ids == win
        vals.append(m)
        idxs.append(win)
        if payload is not None:
            pays.append(jnp.max(jnp.where(hit, payload, -1), axis=0, keepdims=True))
        s = jnp.where(hit, NEG_INF, s)
    cat = lambda xs: jnp.concatenate(xs, axis=0)
    return cat(vals), cat(idxs), (cat(pays) if payload is not None else None)


def _topk_kernel(tt, sc_ref, idx_ref, gate_ref):
    k = PEER_TOPK
    key_ids = lax.broadcasted_iota(jnp.int32, (PEER_KEYS, tt), 0)
    blocks = [(a, a + 1, 0, k // (a + 1)) for a in range(k // 2)] + [(k // 2, k, 0, 1)]
    n_cand = sum((a1 - a0) * (b1 - b0) for a0, a1, b0, b1 in blocks)
    n_pad = -n_cand % SUBLANES
    for hd in range(PEER_HEADS):
        v0, i0, _ = _top_rounds(sc_ref[2 * hd], key_ids, None, k)
        v1, i1, _ = _top_rounds(sc_ref[2 * hd + 1], key_ids, None, k)
        cand, cid, eid = [], [], []
        for (a0, a1, b0, b1) in blocks:
            na, nb = a1 - a0, b1 - b0
            if na == 1:
                va, ia, vb, ib = v0[a0:a1], i0[a0:a1], v1[b0:b1], i1[b0:b1]
                rank = a0 * k + b0 + lax.broadcasted_iota(jnp.int32, (nb, tt), 0)
            else:
                va, ia, vb, ib = v0[a0:a1], i0[a0:a1], v1[b0:b1], i1[b0:b1]
                rank = (a0 + lax.broadcasted_iota(jnp.int32, (na, tt), 0)) * k + b0
            cand.append(va + vb)
            eid.append(ia * PEER_KEYS + ib)
            cid.append(rank)
        if n_pad:
            cand.append(jnp.full((n_pad, tt), NEG_INF, jnp.float32))
            eid.append(jnp.zeros((n_pad, tt), jnp.int32))
            cid.append(jnp.full((n_pad, tt), k * k, jnp.int32))
        cand = jnp.concatenate(cand, axis=0)
        cid = jnp.concatenate(cid, axis=0)
        eid = jnp.concatenate(eid, axis=0)
        ts, _, te = _top_rounds(cand, cid, eid, k)
        e = jnp.exp(ts - ts[0:1])
        g = e / jnp.sum(e, axis=0, keepdims=True)
        idx_ref[hd * k:(hd + 1) * k, :] = te * PACKED_ROWS
        gate_ref[hd * k:(hd + 1) * k, :] = g


def _topk(scores):
    t = scores.shape[2]
    tt = LANES
    return pl.pallas_call(
        functools.partial(_topk_kernel, tt),
        grid=(t // tt,),
        in_specs=[pl.BlockSpec((GROUPS, PEER_KEYS, tt), lambda i: (0, 0, i))],
        out_specs=(pl.BlockSpec((N_SEL, tt), lambda i: (0, i)),
                   pl.BlockSpec((N_SEL, tt), lambda i: (0, i))),
        out_shape=(jax.ShapeDtypeStruct((N_SEL, t), jnp.int32),
                   jax.ShapeDtypeStruct((N_SEL, t), jnp.float32)),
        compiler_params=_cparams("arbitrary"),
        name="topk",
    )(scores)


TOK_BLOCK = 128
TOK_GROUP = 8
N_SLOTS = 4
PACKED_ROWS = ROWS_PER_EXPERT // 2
PACK_BLOCK = 8192


def _split_bf16(x):
    hi = _bf(x)
    lo = _bf(x - hi.astype(jnp.float32))
    return hi, lo


def _pack_kernel(x_ref, o_ref):
    o_ref[...] = pltpu.bitcast(_bf(x_ref[...]), jnp.int32)


def _pack_table(tab):
    rows = tab.size // LANES
    return pl.pallas_call(
        _pack_kernel,
        grid=(rows // PACK_BLOCK,),
        in_specs=[pl.BlockSpec((PACK_BLOCK, LANES), lambda i: (i, 0))],
        out_specs=pl.BlockSpec((PACK_BLOCK // 2, LANES), lambda i: (i, 0)),
        out_shape=jax.ShapeDtypeStruct((rows // 2, LANES), jnp.int32),
        compiler_params=_cparams("arbitrary"),
        name="pack_table",
    )(tab.reshape(rows, LANES))


def _gather_tile(tab_ref, off):
    w = tab_ref[pl.ds(pl.multiple_of(off, PACKED_ROWS), PACKED_ROWS), :]
    return pltpu.bitcast(w, jnp.bfloat16).astype(jnp.float32)


def _token_tile(xg, tl):
    return jnp.concatenate(
        [xg[tl:tl + 1, r * LANES:(r + 1) * LANES] for r in range(ROWS_PER_EXPERT)], axis=0)


def _staged_groups(idx_ref, slots, sem, process):
    n_slots = len(slots)
    n_groups = idx_ref.shape[0] // TOK_GROUP
    assert n_slots % 2 == 0 and n_groups % n_slots == 0

    def copy(g, s):
        rows = pl.ds(pl.multiple_of(g * TOK_GROUP, TOK_GROUP), TOK_GROUP)
        return pltpu.make_async_copy(idx_ref.at[rows, :], slots[s], sem.at[s])

    for s in range(n_slots - 1):
        copy(s, s).start()

    def ring(i, carry):
        for s in range(n_slots):
            g = n_slots * i + s
            copy(g, s).wait()
            ahead = n_slots - 1

            @pl.when(g + ahead < n_groups)
            def _():
                copy(g + ahead, (s + ahead) % n_slots).start()

            process(g, s % 2, slots[s])
        return carry

    lax.fori_loop(0, n_groups // n_slots, ring, 0)
    return n_groups


FOLD_BLOCK = SUBLANES
P_PITCH = ROWS_PER_EXPERT + 1


def _peer_u_kernel(idx_ref, x_ref, gate_ref, tab_ref, out_ref, *scratch):
    slots, sem = scratch[:N_SLOTS], scratch[N_SLOTS]
    f0_ref, f1_ref = scratch[N_SLOTS + 1:N_SLOTS + 3]
    p_refs = scratch[N_SLOTS + 3:]
    ones = jnp.ones((SUBLANES, LANES), jnp.bfloat16)
    sub = lax.broadcasted_iota(jnp.int32, (SUBLANES, LANES), 0)
    f_refs = (f0_ref, f1_ref)

    def lane_fold(f_ref, tl, rows):
        f_hi, f_lo = _split_bf16(f_ref[tl * N_SEL:(tl + 1) * N_SEL, :])
        a = (lax.dot_general(ones, f_hi, _NT, preferred_element_type=jnp.float32)
             + lax.dot_general(ones, f_lo, _NT, preferred_element_type=jnp.float32))
        return a if rows is None else jnp.where(sub == tl, a, rows)

    def store_rows(group, rows):
        out_ref[pl.ds(pl.multiple_of(group * TOK_GROUP, TOK_GROUP), TOK_GROUP), :] = rows

    def process(g, s, slot):
        xg = x_ref[pl.ds(pl.multiple_of(g * TOK_GROUP, TOK_GROUP), TOK_GROUP), :]
        rows = None
        for tl in range(TOK_GROUP + 1):
            if tl < TOK_GROUP:
                xt = _token_tile(xg, tl)
                rows = lane_fold(f_refs[1 - s], tl, rows)
            for kb in range(N_SEL // FOLD_BLOCK):
                if tl < TOK_GROUP:
                    for k in range(kb * FOLD_BLOCK, (kb + 1) * FOLD_BLOCK):
                        p_refs[tl][k * P_PITCH:k * P_PITCH + ROWS_PER_EXPERT, :] = (
                            _gather_tile(tab_ref, slot[tl, k]) * xt)
                if tl >= 1:
                    p_ref = p_refs[tl - 1]
                    first = kb * FOLD_BLOCK * P_PITCH
                    parts = [p_ref[pl.ds(first + r, FOLD_BLOCK, stride=P_PITCH), :]
                             for r in range(ROWS_PER_EXPERT)]
                    while len(parts) > 1:
                        parts = [a + b for a, b in zip(parts[0::2], parts[1::2])]
                    row = (tl - 1) * N_SEL + kb * FOLD_BLOCK
                    f_refs[s][row:row + FOLD_BLOCK, :] = parts[0]
        store_rows(jnp.maximum(g - 1, 0), rows)

    f1_ref[...] = jnp.zeros(f1_ref.shape, jnp.float32)
    n_groups = _staged_groups(idx_ref, slots, sem, process)
    rows = None
    for tl in range(TOK_GROUP):
        rows = lane_fold(f_refs[(n_groups - 1) % 2], tl, rows)
    store_rows(n_groups - 1, rows)
    a = out_ref[...]
    out_ref[...] = gate_ref[...] * (0.5 * a * (1.0 + lax.erf(a * (2.0 ** -0.5))))


def _stage_scratch():
    return [pltpu.SMEM((TOK_GROUP, N_SEL), jnp.int32)] * N_SLOTS + [pltpu.SemaphoreType.DMA((N_SLOTS,))]


def _peer_u(idx, xn, gates, tab):
    t = idx.shape[0]
    tb = min(TOK_BLOCK, t)
    sel = pl.BlockSpec((tb, N_SEL), lambda i: (i, 0))
    return pl.pallas_call(
        _peer_u_kernel,
        grid=(t // tb,),
        in_specs=[sel, pl.BlockSpec((tb, D_MODEL), lambda i: (i, 0)), sel,
                  pl.BlockSpec(memory_space=pltpu.VMEM)],
        out_specs=sel,
        out_shape=jax.ShapeDtypeStruct((t, N_SEL), jnp.float32),
        scratch_shapes=_stage_scratch()
        + [pltpu.VMEM((TOK_GROUP * N_SEL, LANES), jnp.float32)] * 2
        + [pltpu.VMEM((N_SEL * P_PITCH, LANES), jnp.float32)] * TOK_GROUP,
        compiler_params=_cparams("arbitrary"),
        name="peer_u",
    )(idx, xn, gates, tab)


def _peer_v_kernel(idx_ref, w_ref, tab_ref, out_ref, *scratch):
    slots, sem = scratch[:N_SLOTS], scratch[N_SLOTS]
    wb0_ref, wb1_ref, acc_ref = scratch[N_SLOTS + 1:]
    ones = jnp.ones((LANES, LANES), jnp.bfloat16)
    eye = (lax.broadcasted_iota(jnp.int32, (N_SEL, LANES), 0)
           == lax.broadcasted_iota(jnp.int32, (N_SEL, LANES), 1)).astype(jnp.float32)
    wb_refs = (wb0_ref, wb1_ref)
    last_group = idx_ref.shape[0] // TOK_GROUP - 1

    def group_weights(group):
        return w_ref[pl.ds(pl.multiple_of(group * TOK_GROUP, TOK_GROUP), TOK_GROUP), :]

    def broadcast_weights(wb_ref, wg, tl):
        d_hi, d_lo = _split_bf16(eye * wg[tl:tl + 1, :])
        wb_ref[tl * N_SEL:(tl + 1) * N_SEL, :] = (
            jnp.dot(d_hi, ones, preferred_element_type=jnp.float32)
            + jnp.dot(d_lo, ones, preferred_element_type=jnp.float32))

    def process(g, s, slot):
        wg_next = group_weights(jnp.minimum(g + 1, last_group))
        wb_ref = wb_refs[s]
        for tl in range(TOK_GROUP):
            broadcast_weights(wb_refs[1 - s], wg_next, tl)
            acc = acc_ref.at[tl * ROWS_PER_EXPERT:(tl + 1) * ROWS_PER_EXPERT, :]
            for kb in range(N_SEL // FOLD_BLOCK):
                part = None
                for k in range(kb * FOLD_BLOCK, (kb + 1) * FOLD_BLOCK):
                    row = tl * N_SEL + k
                    w = jnp.broadcast_to(wb_ref[row:row + 1, :], (ROWS_PER_EXPERT, LANES))
                    term = w * _gather_tile(tab_ref, slot[tl, k])
                    part = term if part is None else part + term
                acc[...] = part if kb == 0 else acc[...] + part
        tiles = [acc_ref[tl * ROWS_PER_EXPERT:(tl + 1) * ROWS_PER_EXPERT, :] for tl in range(TOK_GROUP)]
        base = pl.multiple_of(g * TOK_GROUP, TOK_GROUP)
        for r in range(ROWS_PER_EXPERT):
            out_ref[pl.ds(base, TOK_GROUP), r * LANES:(r + 1) * LANES] = jnp.concatenate(
                [tile[r:r + 1, :] for tile in tiles], axis=0)

    wg0 = group_weights(0)
    for tl in range(TOK_GROUP):
        broadcast_weights(wb0_ref, wg0, tl)
    _staged_groups(idx_ref, slots, sem, process)


def _peer_v(idx, w, tab):
    t = idx.shape[0]
    tb = min(TOK_BLOCK, t)
    sel = pl.BlockSpec((tb, N_SEL), lambda i: (i, 0))
    return pl.pallas_call(
        _peer_v_kernel,
        grid=(t // tb,),
        in_specs=[sel, sel, pl.BlockSpec(memory_space=pltpu.VMEM)],
        out_specs=pl.BlockSpec((tb, D_MODEL), lambda i: (i, 0)),
        out_shape=jax.ShapeDtypeStruct((t, D_MODEL), jnp.float32),
        scratch_shapes=_stage_scratch() + [pltpu.VMEM((TOK_GROUP * N_SEL, LANES), jnp.float32)] * 2
        + [pltpu.VMEM((TOK_GROUP * ROWS_PER_EXPERT, LANES), jnp.float32)],
        compiler_params=_cparams("arbitrary"),
        name="peer_v",
    )(idx, w, tab)


def _final_kernel(xp_ref, o_ref, g_ref, y_ref):
    y_ref[...] = _rms(xp_ref[...] + o_ref[...], g_ref[...])


def _final(xp, o, g):
    t = xp.shape[0]
    tm = min(512, t)
    tok = pl.BlockSpec((tm, D_MODEL), lambda i: (i, 0))
    return pl.pallas_call(
        _final_kernel,
        grid=(t // tm,),
        in_specs=[tok, tok, pl.BlockSpec((1, D_MODEL), lambda i: (0, 0))],
        out_specs=tok,
        out_shape=jax.ShapeDtypeStruct((t, D_MODEL), jnp.float32),
        compiler_params=_cparams("arbitrary"),
        name="final",
    )(xp, o, g)


def _rope_table(pos):
    inv = ROPE_THETA ** (-jnp.arange(0, QK_ROPE, 2, dtype=jnp.float32) / QK_ROPE)
    ang = pos.astype(jnp.float32)[:, None] * inv[None, :]
    c, s = jnp.cos(ang), jnp.sin(ang)
    return jnp.concatenate([c, c, -s, s], axis=-1)


def _swap_halves(w):
    half = w.shape[-1] // 2
    return jnp.concatenate([w[..., half:], w[..., :half]], axis=-1)


def _prepare_weights(g_mix_norm, w_in, g_q, w_uq, g_kv, w_uk, w_uv, conv_w, g_attn_out, g_conv_out,
                     w_o, g_ffn_norm, w_query, sub_keys, expert_u, expert_v):
    o1 = Q_RANK
    o2 = o1 + KV_RANK
    o3 = o2 + QK_ROPE
    w_kr = w_in[:, o2:o3]
    w_inx = jnp.concatenate([w_in[:, :o3], _swap_halves(w_kr), w_in[:, o3:]], axis=1)
    q_nope = w_uq[:, :, :QK_NOPE].reshape(Q_RANK, N_HEADS * QK_NOPE)
    q_rope = w_uq[:, :, QK_NOPE:]
    q_pair = jnp.concatenate([q_rope, _swap_halves(q_rope)], axis=-1).reshape(Q_RANK, N_HEADS * LANES)
    row = lambda g: g.reshape(1, -1)
    return {
        "g_mix": row(g_mix_norm), "w_in": _bf(w_inx), "g_q": row(g_q),
        "w_q": _bf(jnp.concatenate([q_nope, q_pair], axis=1)),
        "w_uk": _bf(jnp.transpose(w_uk, (1, 2, 0))), "g_kv": row(g_kv),
        "w_uv": _bf(jnp.transpose(w_uv, (1, 0, 2))), "conv_w": conv_w,
        "g_attn": row(g_attn_out), "g_conv": row(g_conv_out), "w_o": _bf(w_o),
        "g_ffn": row(g_ffn_norm), "w_query": _bf(w_query), "sub_keys": _bf(sub_keys),
        "tab_u": _pack_table(expert_u), "tab_v": _pack_table(expert_v),
    }


def _channel_mixer(attn, conv, x, wts, g_final):
    t = x.shape[0]
    xp, xn, scores = _mix(attn, conv, x, wts)
    idx_t, gate_t = _topk(scores)
    idx, gates = idx_t.T, gate_t.T
    w = _peer_u(idx, xn, gates, wts["tab_u"])
    o = _peer_v(idx, w, wts["tab_v"])
    return _final(xp, o, g_final.reshape(1, D_MODEL))


def kernel(x_prompt, x_sample, cache_ckv, cache_krope, state_conv, page_table, g_mix_norm, w_in, g_q, w_uq, g_kv, w_uk, w_uv, conv_w, g_attn_out, g_conv_out, w_o, g_ffn_norm, w_query, sub_keys, expert_u, expert_v, g_final):
    assert w_in.shape[0] == 1, "single-layer step"
    wts = _prepare_weights(g_mix_norm[0], w_in[0], g_q[0], w_uq[0], g_kv[0], w_uk[0], w_uv[0], conv_w[0],
                           g_attn_out[0], g_conv_out[0], w_o[0], g_ffn_norm[0], w_query[0], sub_keys[0],
                           expert_u[0], expert_v[0])
    b, s, _ = x_prompt.shape
    nb, n_pages = page_table.shape
    past = n_pages * PAGE_SIZE

    zero_prev = jnp.zeros((SUBLANES, CONV_WIDTH), jnp.float32)
    qcat, kcat, ckv_p, kr_p, conv_p, tail_p = _proj(
        x_prompt, _rope_table(jnp.arange(s)), zero_prev, zero_prev, wts, True)
    attn_p = _flash(qcat, kcat, wts)
    y_prompt = _channel_mixer(attn_p.reshape(b * s, ATTN_WIDTH), conv_p.reshape(b * s, CONV_WIDTH),
                              x_prompt.reshape(b * s, D_MODEL), wts, g_final).reshape(b, s, D_MODEL)
    new_conv_p = tail_p[:, SUBLANES - (CONV_K - 1):, :]

    xs = x_sample.reshape(1, nb, D_MODEL)
    cs_s = jnp.broadcast_to(_rope_table(jnp.full((1,), past)), (nb, LANES))
    s0, s1 = state_conv[0, :, 0, :], state_conv[0, :, 1, :]
    qcat_s, kcat_s, ckv_s, kr_s, conv_s, z_s = _proj(xs, cs_s, s0, s1, wts, False)
    q_dec = jnp.pad(jnp.transpose(qcat_s[0], (1, 0, 2)), ((0, 0), (0, Q_ROWS - N_HEADS), (0, 0)))
    o_lat = _decode(page_table, q_dec, kcat_s.reshape(nb, 1, QK_PAD), cache_ckv,
                    jnp.swapaxes(cache_krope, 2, 3))
    attn_s = _attn_out_call(jnp.transpose(o_lat[:, :N_HEADS], (1, 0, 2)), wts)
    y_sample = _channel_mixer(attn_s, conv_s[0], x_sample.reshape(nb, D_MODEL), wts, g_final)
    new_conv_s = jnp.stack([s1, z_s[0]], axis=1)

    return (y_prompt, y_sample.reshape(nb, 1, D_MODEL),
            ckv_p[None], kr_p[None], new_conv_p[None],
            ckv_s.reshape(1, nb, 1, KV_RANK), kr_s.reshape(1, nb, 1, QK_ROPE), new_conv_s[None])
```

```python
import functools

import jax
import jax.numpy as jnp
import numpy as np
from jax import lax
from jax.experimental import pallas as pl
from jax.experimental.pallas import tpu as pltpu

D_MODEL = 1024
N_HEADS = 4
QK_NOPE = 128
QK_ROPE = 64
V_DIM = 128
Q_RANK = 384
KV_RANK = 256
ATTN_WIDTH = N_HEADS * V_DIM
CONV_WIDTH = D_MODEL - ATTN_WIDTH
CONV_GROUPS = 4
CONV_K = 3
PEER_HEADS = 8
PEER_KEYS = 128
PEER_DK = 256
PEER_TOPK = 16
PAGE_SIZE = 128
ROPE_THETA = 10000.0
EPS = 1e-6
SM_SCALE = (QK_NOPE + QK_ROPE) ** -0.5

LANES = 128
SUBLANES = 8
VMEM_LIMIT = 48 * 1024 * 1024

QK_PAD = 384
N_SEL = PEER_HEADS * PEER_TOPK
ROWS_PER_EXPERT = D_MODEL // LANES
GROUPS = 2 * PEER_HEADS
D_INX = Q_RANK + KV_RANK + 2 * QK_ROPE + 3 * CONV_WIDTH
NEG_INF = float("-inf")

_NT = (((1,), (1,)), ((), ()))


def _cparams(*sem):
    return pltpu.CompilerParams(dimension_semantics=sem, vmem_limit_bytes=VMEM_LIMIT)


def _rms(x, g):
    return x * lax.rsqrt(jnp.mean(x * x, axis=-1, keepdims=True) + EPS) * g


def _bf(x):
    return x.astype(jnp.bfloat16)


def _rope_pair(t, cs):
    r = t * cs
    return r + pltpu.roll(r, QK_ROPE, axis=1)


def _proj_kernel(seq_conv, tm, x_ref, gmix_ref, win_ref, gq_ref, wq_ref, wuk_ref, gkv_ref,
                 convw_ref, cs_ref, s0_ref, s1_ref, gconv_ref,
                 qcat_ref, kcat_ref, ckv_ref, kr_ref, convn_ref, z_ref, zbuf_ref):
    si = pl.program_id(1)
    x = x_ref[0]
    xn = _bf(_rms(x, gmix_ref[...]))
    h = jnp.dot(xn, win_ref[...], preferred_element_type=jnp.float32)
    o1 = Q_RANK
    o2 = o1 + KV_RANK
    o3 = o2 + 2 * QK_ROPE
    o4 = o3 + CONV_WIDTH
    o5 = o4 + CONV_WIDTH
    cs = cs_ref[...]
    lane = lax.broadcasted_iota(jnp.int32, (tm, LANES), 1)
    rope_mask = lane < QK_ROPE

    ckv = _rms(h[:, o1:o2], gkv_ref[...])
    kr = _rope_pair(h[:, o2:o3], cs)
    ckv_ref[0] = ckv
    kr_ref[0] = kr[:, :QK_ROPE]
    kcat_ref[0, :, 0:KV_RANK] = _bf(ckv)
    kcat_ref[0, :, KV_RANK:QK_PAD] = _bf(jnp.where(rope_mask, kr, 0.0))

    cq = _bf(_rms(h[:, 0:o1], gq_ref[...]))
    q = jnp.dot(cq, wq_ref[...], preferred_element_type=jnp.float32)
    for hd in range(N_HEADS):
        qn = _bf(q[:, hd * QK_NOPE:(hd + 1) * QK_NOPE])
        qlat = jnp.dot(qn, wuk_ref[hd], preferred_element_type=jnp.float32)
        base = N_HEADS * QK_NOPE + hd * LANES
        qr = _rope_pair(q[:, base:base + LANES], cs)
        qcat_ref[0, hd, :, 0:KV_RANK] = _bf(qlat * SM_SCALE)
        qcat_ref[0, hd, :, KV_RANK:QK_PAD] = _bf(jnp.where(rope_mask, qr * SM_SCALE, 0.0))

    z = h[:, o4:o5] * h[:, o3:o4]
    zb = h[:, o5:]
    w0 = convw_ref[0:1, :]
    w1 = convw_ref[1:2, :]
    w2 = convw_ref[2:3, :]
    if seq_conv:
        @pl.when(si == 0)
        def _():
            zbuf_ref[0:SUBLANES, :] = jnp.zeros((SUBLANES, CONV_WIDTH), jnp.float32)

        zbuf_ref[SUBLANES:SUBLANES + tm, :] = z
        y = (w2 * z + w1 * zbuf_ref[SUBLANES - 1:SUBLANES - 1 + tm, :]
             + w0 * zbuf_ref[SUBLANES - 2:SUBLANES - 2 + tm, :])
        tail = zbuf_ref[tm:tm + SUBLANES, :]
        zbuf_ref[0:SUBLANES, :] = tail
        z_ref[0] = tail
    else:
        y = w2 * z + w1 * s1_ref[...] + w0 * s0_ref[...]
        z_ref[0] = z
    c = zb * y
    for g in range(CONV_GROUPS):
        sl = slice(g * LANES, (g + 1) * LANES)
        convn_ref[0, :, sl] = _bf(_rms(c[:, sl], gconv_ref[:, sl]))


def _proj(x, cs, s0, s1, wts, seq_conv):
    b, s, _ = x.shape
    tm = min(512, s)
    grid = (b, s // tm)
    z_rows = SUBLANES if seq_conv else tm
    full = lambda shape: pl.BlockSpec(shape, lambda i, j: (0,) * len(shape))
    tok = lambda width: pl.BlockSpec((1, tm, width), lambda i, j: (i, j, 0))
    prev = pl.BlockSpec((s0.shape[0] if seq_conv else tm, CONV_WIDTH),
                        lambda i, j: (0 if seq_conv else j, 0))
    out_shapes = (
        jax.ShapeDtypeStruct((b, N_HEADS, s, QK_PAD), jnp.bfloat16),
        jax.ShapeDtypeStruct((b, s, QK_PAD), jnp.bfloat16),
        jax.ShapeDtypeStruct((b, s, KV_RANK), jnp.float32),
        jax.ShapeDtypeStruct((b, s, QK_ROPE), jnp.float32),
        jax.ShapeDtypeStruct((b, s, CONV_WIDTH), jnp.bfloat16),
        jax.ShapeDtypeStruct((b, z_rows, CONV_WIDTH), jnp.float32),
    )
    out_specs = (
        pl.BlockSpec((1, N_HEADS, tm, QK_PAD), lambda i, j: (i, 0, j, 0)),
        tok(QK_PAD), tok(KV_RANK), tok(QK_ROPE), tok(CONV_WIDTH),
        pl.BlockSpec((1, z_rows, CONV_WIDTH), lambda i, j: (i, 0, 0)),
    )
    return pl.pallas_call(
        functools.partial(_proj_kernel, seq_conv, tm),
        grid=grid,
        in_specs=[
            tok(D_MODEL), full((1, D_MODEL)), full((D_MODEL, D_INX)), full((1, Q_RANK)),
            full((Q_RANK, 2 * N_HEADS * QK_NOPE)), full((N_HEADS, QK_NOPE, KV_RANK)),
            full((1, KV_RANK)), full((CONV_K, CONV_WIDTH)),
            pl.BlockSpec((tm, LANES), lambda i, j: (j, 0)), prev, prev,
            full((1, CONV_WIDTH)),
        ],
        out_specs=out_specs,
        out_shape=out_shapes,
        scratch_shapes=[pltpu.VMEM((tm + 2 * SUBLANES, CONV_WIDTH), jnp.float32)],
        compiler_params=_cparams("arbitrary", "arbitrary"),
        name="proj_seq" if seq_conv else "proj_tok",
    )(x, wts["g_mix"], wts["w_in"], wts["g_q"], wts["w_q"], wts["w_uk"], wts["g_kv"],
      wts["conv_w"], cs, s0, s1, wts["g_conv"])


def _attn_out(o, wuv_ref, gattn_ref, hd):
    a = jnp.dot(_bf(o), wuv_ref[hd], preferred_element_type=jnp.float32)
    return _bf(_rms(a, gattn_ref[:, hd * V_DIM:(hd + 1) * V_DIM]))


SOFTMAX_ROWS = 128


def _flash_kernel(tq, th, q_ref, k_ref, wuv_ref, gattn_ref, out_ref, m_ref, l_ref, acc_ref,
                  s0_ref, s1_ref, p0_ref, p1_ref, a0_ref, a1_ref):
    qi = pl.program_id(1)
    ki = pl.program_id(2)
    rows = N_HEADS * tq
    last_k = (qi * tq + tq - 1) // (2 * th)

    @pl.when(ki == 0)
    def _():
        m_ref[...] = jnp.full(m_ref.shape, NEG_INF, jnp.float32)
        l_ref[...] = jnp.zeros(l_ref.shape, jnp.float32)
        acc_ref[...] = jnp.zeros(acc_ref.shape, jnp.float32)

    def softmax_half(masked, half, s_ref, p_ref, a_ref):
        for c in range(rows // SOFTMAX_ROWS):
            rs = pl.ds(c * SOFTMAX_ROWS, SOFTMAX_ROWS)
            s = s_ref[rs, :]
            if masked:
                t0 = (c % (tq // SOFTMAX_ROWS)) * SOFTMAX_ROWS
                qpos = qi * tq + t0 + lax.broadcasted_iota(jnp.int32, (SOFTMAX_ROWS, th), 0)
                kpos = (2 * ki + half) * th + lax.broadcasted_iota(jnp.int32, (SOFTMAX_ROWS, th), 1)
                s = jnp.where(kpos <= qpos, s, NEG_INF)
            m_prev = m_ref[rs, :]
            m_new = jnp.maximum(m_prev, jnp.max(s, axis=-1, keepdims=True))
            alpha = jnp.exp(m_prev - m_new)
            p = jnp.exp(s - jnp.concatenate([m_new] * (th // LANES), axis=1))
            l_ref[rs, :] = alpha * l_ref[rs, :] + jnp.sum(p, axis=-1, keepdims=True)
            a_ref[rs, :] = alpha
            p_ref[rs, :] = _bf(p)
            m_ref[rs, :] = m_new

    def step(masked):
        q = q_ref[0].reshape(rows, QK_PAD)
        k0 = k_ref[0, 0:th, :]
        k1 = k_ref[0, th:2 * th, :]
        s0_ref[...] = lax.dot_general(q, k0, _NT, preferred_element_type=jnp.float32)
        s1_ref[...] = lax.dot_general(q, k1, _NT, preferred_element_type=jnp.float32)
        softmax_half(masked, 0, s0_ref, p0_ref, a0_ref)
        pv0 = jnp.dot(p0_ref[...], k0[:, 0:KV_RANK], preferred_element_type=jnp.float32)
        softmax_half(masked, 1, s1_ref, p1_ref, a1_ref)
        pv1 = jnp.dot(p1_ref[...], k1[:, 0:KV_RANK], preferred_element_type=jnp.float32)
        wide = lambda a: jnp.concatenate([a] * (KV_RANK // LANES), axis=1)
        acc_ref[...] = (acc_ref[...] * wide(a0_ref[...]) + pv0) * wide(a1_ref[...]) + pv1

    unmasked = (2 * ki + 2) * th <= qi * tq + 1
    pl.when(unmasked)(functools.partial(step, False))
    pl.when(jnp.logical_not(unmasked) & (ki <= last_k))(functools.partial(step, True))

    @pl.when(ki == last_k)
    def _():
        o = acc_ref[...] / l_ref[:, 0:1]
        for hd in range(N_HEADS):
            out_ref[0, :, hd * V_DIM:(hd + 1) * V_DIM] = _attn_out(
                o[hd * tq:(hd + 1) * tq], wuv_ref, gattn_ref, hd)


def _flash(qcat, kcat, wts):
    b, _, s, _ = qcat.shape
    tq, th = 512, 512
    tk = 2 * th
    rows = N_HEADS * tq
    return pl.pallas_call(
        functools.partial(_flash_kernel, tq, th),
        grid=(b, s // tq, s // tk),
        in_specs=[
            pl.BlockSpec((1, N_HEADS, tq, QK_PAD), lambda i, q, k: (i, 0, q, 0)),
            pl.BlockSpec((1, tk, QK_PAD),
                         lambda i, q, k: (i, jnp.minimum(k, (q * tq + tq - 1) // tk), 0)),
            pl.BlockSpec((N_HEADS, KV_RANK, V_DIM), lambda i, q, k: (0, 0, 0)),
            pl.BlockSpec((1, ATTN_WIDTH), lambda i, q, k: (0, 0)),
        ],
        out_specs=pl.BlockSpec((1, tq, ATTN_WIDTH), lambda i, q, k: (i, q, 0)),
        out_shape=jax.ShapeDtypeStruct((b, s, ATTN_WIDTH), jnp.bfloat16),
        scratch_shapes=[pltpu.VMEM((rows, LANES), jnp.float32),
                        pltpu.VMEM((rows, LANES), jnp.float32),
                        pltpu.VMEM((rows, KV_RANK), jnp.float32),
                        pltpu.VMEM((rows, th), jnp.float32), pltpu.VMEM((rows, th), jnp.float32),
                        pltpu.VMEM((rows, th), jnp.bfloat16), pltpu.VMEM((rows, th), jnp.bfloat16),
                        pltpu.VMEM((rows, LANES), jnp.float32), pltpu.VMEM((rows, LANES), jnp.float32)],
        compiler_params=_cparams("arbitrary", "arbitrary", "arbitrary"),
        name="flash",
    )(qcat, kcat, wts["w_uv"], wts["g_attn"])


PAGES_PER_STEP = 16
Q_ROWS = 16


def _decode_kernel(n_steps, pt_ref, q_ref, knew_ref, ckv_hbm, kr_hbm, out_ref,
                   ckv_buf, kr_buf, sem, m_ref, l_ref, acc_ref):
    b = pl.program_id(0)
    j = pl.program_id(1)
    total_steps = pl.num_programs(0) * n_steps
    step = b * n_steps + j
    slot = step % 2

    def page_copies(page_of, sl):
        copies = []
        for i in range(PAGES_PER_STEP):
            page = page_of(i)
            copies.append(pltpu.make_async_copy(ckv_hbm.at[0, page], ckv_buf.at[sl, i], sem.at[0, sl]))
            copies.append(pltpu.make_async_copy(kr_hbm.at[0, page], kr_buf.at[sl, i], sem.at[1, sl]))
        return copies

    @pl.when(step == 0)
    def _():
        for c in page_copies(lambda i: pt_ref[0, i], 0):
            c.start()

    nxt = step + 1

    @pl.when(nxt < total_steps)
    def _():
        nb_, nj_ = nxt // n_steps, nxt % n_steps
        for c in page_copies(lambda i: pt_ref[nb_, nj_ * PAGES_PER_STEP + i], 1 - slot):
            c.start()

    for c in page_copies(lambda i: 0, slot):
        c.wait()

    @pl.when(j == 0)
    def _():
        m_ref[...] = jnp.full(m_ref.shape, NEG_INF, jnp.float32)
        l_ref[...] = jnp.zeros(l_ref.shape, jnp.float32)
        acc_ref[...] = jnp.zeros(acc_ref.shape, jnp.float32)

    q = q_ref[0]
    qlat = q[:, 0:KV_RANK]
    qrope = q[:, KV_RANK:KV_RANK + QK_ROPE]
    vals = [_bf(ckv_buf[slot, i]) for i in range(PAGES_PER_STEP)]
    s = jnp.concatenate(
        [lax.dot_general(qlat, v, _NT, preferred_element_type=jnp.float32)
         + jnp.dot(qrope, _bf(kr_buf[slot, i]), preferred_element_type=jnp.float32)
         for i, v in enumerate(vals)], axis=-1)
    m_prev = m_ref[...]
    m_new = jnp.maximum(m_prev, jnp.max(s, axis=-1, keepdims=True))
    alpha = jnp.exp(m_prev - m_new)
    p = jnp.exp(s - m_new[:, 0:1])
    l_new = alpha * l_ref[...] + jnp.sum(p, axis=-1, keepdims=True)
    acc = acc_ref[...] * alpha[:, 0:1]
    for i, v in enumerate(vals):
        acc = acc + jnp.dot(_bf(p[:, i * PAGE_SIZE:(i + 1) * PAGE_SIZE]), v,
                            preferred_element_type=jnp.float32)
    m_ref[...] = m_new
    l_ref[...] = l_new
    acc_ref[...] = acc

    @pl.when(j == n_steps - 1)
    def _():
        kn = knew_ref[0].astype(jnp.float32)
        s_new = jnp.sum(q.astype(jnp.float32) * kn, axis=-1, keepdims=True)
        m_fin = jnp.maximum(m_new, s_new)
        a2 = jnp.exp(m_new - m_fin)
        p_new = jnp.exp(s_new - m_fin[:, 0:1])
        l_fin = a2 * l_new + p_new
        acc_fin = acc * a2[:, 0:1] + _bf(p_new).astype(jnp.float32) * kn[:, 0:KV_RANK]
        out_ref[0] = acc_fin / l_fin[:, 0:1]


def _decode(page_table, q, knew, cache_ckv, cache_krope_t):
    nb, n_pages = page_table.shape
    n_steps = n_pages // PAGES_PER_STEP

    grid_spec = pltpu.PrefetchScalarGridSpec(
        num_scalar_prefetch=1,
        grid=(nb, n_steps),
        in_specs=[pl.BlockSpec((1, Q_ROWS, QK_PAD), lambda b, j, pt: (b, 0, 0)),
                  pl.BlockSpec((1, 1, QK_PAD), lambda b, j, pt: (b, 0, 0)),
                  pl.BlockSpec(memory_space=pl.ANY), pl.BlockSpec(memory_space=pl.ANY)],
        out_specs=pl.BlockSpec((1, Q_ROWS, KV_RANK), lambda b, j, pt: (b, 0, 0)),
        scratch_shapes=[pltpu.VMEM((2, PAGES_PER_STEP, PAGE_SIZE, KV_RANK), jnp.float32),
                        pltpu.VMEM((2, PAGES_PER_STEP, QK_ROPE, PAGE_SIZE), jnp.float32),
                        pltpu.SemaphoreType.DMA((2, 2)),
                        pltpu.VMEM((Q_ROWS, LANES), jnp.float32),
                        pltpu.VMEM((Q_ROWS, LANES), jnp.float32),
                        pltpu.VMEM((Q_ROWS, KV_RANK), jnp.float32)],
    )
    return pl.pallas_call(
        functools.partial(_decode_kernel, n_steps),
        grid_spec=grid_spec,
        out_shape=jax.ShapeDtypeStruct((nb, Q_ROWS, KV_RANK), jnp.float32),
        compiler_params=_cparams("arbitrary", "arbitrary"),
        name="decode",
    )(page_table, q, knew, cache_ckv, cache_krope_t)


def _attn_out_kernel(o_ref, wuv_ref, gattn_ref, out_ref):
    for hd in range(N_HEADS):
        out_ref[:, hd * V_DIM:(hd + 1) * V_DIM] = _attn_out(o_ref[hd], wuv_ref, gattn_ref, hd)


def _attn_out_call(o, wts):
    t = o.shape[1]
    return pl.pallas_call(
        _attn_out_kernel,
        out_shape=jax.ShapeDtypeStruct((t, ATTN_WIDTH), jnp.bfloat16),
        name="attn_out",
    )(o, wts["w_uv"], wts["g_attn"])


def _mix_kernel(attn_ref, conv_ref, x_ref, wo_ref, gffn_ref, wqry_ref, keys_ref,
                xp_ref, xn_ref, sc_ref):
    y = (jnp.dot(attn_ref[...], wo_ref[0:ATTN_WIDTH, :], preferred_element_type=jnp.float32)
         + jnp.dot(conv_ref[...], wo_ref[ATTN_WIDTH:, :], preferred_element_type=jnp.float32))
    xp = x_ref[...] + y
    xp_ref[...] = xp
    xn = _rms(xp, gffn_ref[...])
    xn_ref[...] = xn
    q = jnp.dot(_bf(xn), wqry_ref[...], preferred_element_type=jnp.float32)
    half = PEER_DK // 2
    for g in range(GROUPS):
        sc_ref[g] = lax.dot_general(keys_ref[g % 2], _bf(q[:, g * half:(g + 1) * half]), _NT,
                                    preferred_element_type=jnp.float32)


def _mix(attn, conv, x, wts):
    t = x.shape[0]
    tm = min(256, t)
    full = lambda shape: pl.BlockSpec(shape, lambda i: (0,) * len(shape))
    tok = lambda width: pl.BlockSpec((tm, width), lambda i: (i, 0))
    return pl.pallas_call(
        _mix_kernel,
        grid=(t // tm,),
        in_specs=[tok(ATTN_WIDTH), tok(CONV_WIDTH), tok(D_MODEL), full((D_MODEL, D_MODEL)),
                  full((1, D_MODEL)), full((D_MODEL, PEER_HEADS * PEER_DK)),
                  full((2, PEER_KEYS, PEER_DK // 2))],
        out_specs=(tok(D_MODEL), tok(D_MODEL),
                   pl.BlockSpec((GROUPS, PEER_KEYS, tm), lambda i: (0, 0, i))),
        out_shape=(jax.ShapeDtypeStruct((t, D_MODEL), jnp.float32),
                   jax.ShapeDtypeStruct((t, D_MODEL), jnp.float32),
                   jax.ShapeDtypeStruct((GROUPS, PEER_KEYS, t), jnp.float32)),
        compiler_params=_cparams("arbitrary"),
        name="mix",
    )(attn, conv, x, wts["w_o"], wts["g_ffn"], wts["w_query"], wts["sub_keys"])


def _top_rounds(s, ids, payload, n):
    vals, idxs, pays = [], [], []
    sentinel = jnp.int32(2 ** 30)
    for r in range(n):
        m = jnp.max(s, axis=0, keepdims=True)
        win = jnp.min(jnp.where(s == m, ids, sentinel), axis=0, keepdims=True)
        vals.append(m)
        idxs.append(win)
        if payload is not None:
            pays.append(jnp.max(jnp.where(ids == win, payload, -1), axis=0, keepdims=True))
        if r < n - 1:
            s = jnp.where(ids == win, NEG_INF, s)
    cat = lambda xs: jnp.concatenate(xs, axis=0)
    return cat(vals), cat(idxs), (cat(pays) if payload is not None else None)


def _topk_kernel(tt, sc_ref, idx_ref, gate_ref):
    k = PEER_TOPK
    key_ids = lax.broadcasted_iota(jnp.int32, (PEER_KEYS, tt), 0)
    blocks = [(a, a + 1, 0, k // (a + 1)) for a in range(k // 2)] + [(k // 2, k, 0, 1)]
    n_cand = sum((a1 - a0) * (b1 - b0) for a0, a1, b0, b1 in blocks)
    n_pad = -n_cand % SUBLANES
    for hd in range(PEER_HEADS):
        v0, i0, _ = _top_rounds(sc_ref[2 * hd], key_ids, None, k)
        v1, i1, _ = _top_rounds(sc_ref[2 * hd + 1], key_ids, None, k)
        cand, cid, eid = [], [], []
        for (a0, a1, b0, b1) in blocks:
            na, nb = a1 - a0, b1 - b0
            if na == 1:
                va, ia, vb, ib = v0[a0:a1], i0[a0:a1], v1[b0:b1], i1[b0:b1]
                rank = a0 * k + b0 + lax.broadcasted_iota(jnp.int32, (nb, tt), 0)
            else:
                va, ia, vb, ib = v0[a0:a1], i0[a0:a1], v1[b0:b1], i1[b0:b1]
                rank = (a0 + lax.broadcasted_iota(jnp.int32, (na, tt), 0)) * k + b0
            cand.append(va + vb)
            eid.append(ia * PEER_KEYS + ib)
            cid.append(rank)
        if n_pad:
            cand.append(jnp.full((n_pad, tt), NEG_INF, jnp.float32))
            eid.append(jnp.zeros((n_pad, tt), jnp.int32))
            cid.append(jnp.full((n_pad, tt), k * k, jnp.int32))
        cand = jnp.concatenate(cand, axis=0)
        cid = jnp.concatenate(cid, axis=0)
        eid = jnp.concatenate(eid, axis=0)
        ts, _, te = _top_rounds(cand, cid, eid, k)
        e = jnp.exp(ts - ts[0:1])
        g = e / jnp.sum(e, axis=0, keepdims=True)
        idx_ref[hd * k:(hd + 1) * k, :] = te * PACKED_ROWS
        gate_ref[hd * k:(hd + 1) * k, :] = g


def _topk(scores):
    t = scores.shape[2]
    tt = LANES
    return pl.pallas_call(
        functools.partial(_topk_kernel, tt),
        grid=(t // tt,),
        in_specs=[pl.BlockSpec((GROUPS, PEER_KEYS, tt), lambda i: (0, 0, i))],
        out_specs=(pl.BlockSpec((N_SEL, tt), lambda i: (0, i)),
                   pl.BlockSpec((N_SEL, tt), lambda i: (0, i))),
        out_shape=(jax.ShapeDtypeStruct((N_SEL, t), jnp.int32),
                   jax.ShapeDtypeStruct((N_SEL, t), jnp.float32)),
        compiler_params=_cparams("arbitrary"),
        name="topk",
    )(scores)


TOK_BLOCK = 128
TOK_GROUP = 8
N_SLOTS = 4
PACKED_ROWS = ROWS_PER_EXPERT // 2
PACK_BLOCK = 8192


def _split_bf16(x):
    hi = _bf(x)
    lo = _bf(x - hi.astype(jnp.float32))
    return hi, lo


def _pack_kernel(x_ref, o_ref):
    o_ref[...] = pltpu.bitcast(_bf(x_ref[...]), jnp.int32)


def _pack_table(tab):
    rows = tab.size // LANES
    return pl.pallas_call(
        _pack_kernel,
        grid=(rows // PACK_BLOCK,),
        in_specs=[pl.BlockSpec((PACK_BLOCK, LANES), lambda i: (i, 0))],
        out_specs=pl.BlockSpec((PACK_BLOCK // 2, LANES), lambda i: (i, 0)),
        out_shape=jax.ShapeDtypeStruct((rows // 2, LANES), jnp.int32),
        compiler_params=_cparams("arbitrary"),
        name="pack_table",
    )(tab.reshape(rows, LANES))


def _gather_tile(tab_ref, off):
    w = tab_ref[pl.ds(pl.multiple_of(off, PACKED_ROWS), PACKED_ROWS), :]
    return pltpu.bitcast(w, jnp.bfloat16).astype(jnp.float32)


def _token_tile(xg, tl):
    return jnp.concatenate(
        [xg[tl:tl + 1, r * LANES:(r + 1) * LANES] for r in range(ROWS_PER_EXPERT)], axis=0)


def _staged_groups(idx_ref, slots, sem, process):
    n_slots = len(slots)
    n_groups = idx_ref.shape[0] // TOK_GROUP
    assert n_slots % 2 == 0 and n_groups % n_slots == 0

    def copy(g, s):
        rows = pl.ds(pl.multiple_of(g * TOK_GROUP, TOK_GROUP), TOK_GROUP)
        return pltpu.make_async_copy(idx_ref.at[rows, :], slots[s], sem.at[s])

    for s in range(n_slots - 1):
        copy(s, s).start()

    def ring(i, carry):
        for s in range(n_slots):
            g = n_slots * i + s
            copy(g, s).wait()
            ahead = n_slots - 1

            @pl.when(g + ahead < n_groups)
            def _():
                copy(g + ahead, (s + ahead) % n_slots).start()

            process(g, s % 2, slots[s])
        return carry

    lax.fori_loop(0, n_groups // n_slots, ring, 0)
    return n_groups


FOLD_BLOCK = SUBLANES
P_PITCH = ROWS_PER_EXPERT + 1


def _peer_u_kernel(idx_ref, x_ref, gate_ref, tab_ref, out_ref, *scratch):
    slots, sem = scratch[:N_SLOTS], scratch[N_SLOTS]
    f0_ref, f1_ref = scratch[N_SLOTS + 1:N_SLOTS + 3]
    p_refs = scratch[N_SLOTS + 3:]
    ones = jnp.ones((SUBLANES, LANES), jnp.bfloat16)
    sub = lax.broadcasted_iota(jnp.int32, (SUBLANES, LANES), 0)
    f_refs = (f0_ref, f1_ref)

    def lane_fold(f_ref, tl, rows):
        f_hi, f_lo = _split_bf16(f_ref[tl * N_SEL:(tl + 1) * N_SEL, :])
        a = (lax.dot_general(ones, f_hi, _NT, preferred_element_type=jnp.float32)
             + lax.dot_general(ones, f_lo, _NT, preferred_element_type=jnp.float32))
        return a if rows is None else jnp.where(sub == tl, a, rows)

    def store_rows(group, rows):
        out_ref[pl.ds(pl.multiple_of(group * TOK_GROUP, TOK_GROUP), TOK_GROUP), :] = rows

    def process(g, s, slot):
        xg = x_ref[pl.ds(pl.multiple_of(g * TOK_GROUP, TOK_GROUP), TOK_GROUP), :]
        rows = None
        for tl in range(TOK_GROUP + 1):
            if tl < TOK_GROUP:
                xt = _token_tile(xg, tl)
                rows = lane_fold(f_refs[1 - s], tl, rows)
            for kb in range(N_SEL // FOLD_BLOCK):
                if tl < TOK_GROUP:
                    for k in range(kb * FOLD_BLOCK, (kb + 1) * FOLD_BLOCK):
                        p_refs[tl][k * P_PITCH:k * P_PITCH + ROWS_PER_EXPERT, :] = (
                            _gather_tile(tab_ref, slot[tl, k]) * xt)
                if tl >= 1:
                    p_ref = p_refs[tl - 1]
                    first = kb * FOLD_BLOCK * P_PITCH
                    parts = [p_ref[pl.ds(first + r, FOLD_BLOCK, stride=P_PITCH), :]
                             for r in range(ROWS_PER_EXPERT)]
                    while len(parts) > 1:
                        parts = [a + b for a, b in zip(parts[0::2], parts[1::2])]
                    row = (tl - 1) * N_SEL + kb * FOLD_BLOCK
                    f_refs[s][row:row + FOLD_BLOCK, :] = parts[0]
        store_rows(jnp.maximum(g - 1, 0), rows)

    f1_ref[...] = jnp.zeros(f1_ref.shape, jnp.float32)
    n_groups = _staged_groups(idx_ref, slots, sem, process)
    rows = None
    for tl in range(TOK_GROUP):
        rows = lane_fold(f_refs[(n_groups - 1) % 2], tl, rows)
    store_rows(n_groups - 1, rows)
    a = out_ref[...]
    out_ref[...] = gate_ref[...] * (0.5 * a * (1.0 + lax.erf(a * (2.0 ** -0.5))))


def _stage_scratch():
    return [pltpu.SMEM((TOK_GROUP, N_SEL), jnp.int32)] * N_SLOTS + [pltpu.SemaphoreType.DMA((N_SLOTS,))]


def _peer_u(idx, xn, gates, tab):
    t = idx.shape[0]
    tb = min(TOK_BLOCK, t)
    sel = pl.BlockSpec((tb, N_SEL), lambda i: (i, 0))
    return pl.pallas_call(
        _peer_u_kernel,
        grid=(t // tb,),
        in_specs=[sel, pl.BlockSpec((tb, D_MODEL), lambda i: (i, 0)), sel,
                  pl.BlockSpec(memory_space=pltpu.VMEM)],
        out_specs=sel,
        out_shape=jax.ShapeDtypeStruct((t, N_SEL), jnp.float32),
        scratch_shapes=_stage_scratch()
        + [pltpu.VMEM((TOK_GROUP * N_SEL, LANES), jnp.float32)] * 2
        + [pltpu.VMEM((N_SEL * P_PITCH, LANES), jnp.float32)] * TOK_GROUP,
        compiler_params=_cparams("arbitrary"),
        name="peer_u",
    )(idx, xn, gates, tab)


def _peer_v_kernel(idx_ref, w_ref, tab_ref, out_ref, *scratch):
    slots, sem = scratch[:N_SLOTS], scratch[N_SLOTS]
    wb0_ref, wb1_ref, acc_ref = scratch[N_SLOTS + 1:]
    ones = jnp.ones((LANES, LANES), jnp.bfloat16)
    eye = (lax.broadcasted_iota(jnp.int32, (N_SEL, LANES), 0)
           == lax.broadcasted_iota(jnp.int32, (N_SEL, LANES), 1)).astype(jnp.float32)
    wb_refs = (wb0_ref, wb1_ref)
    last_group = idx_ref.shape[0] // TOK_GROUP - 1

    def group_weights(group):
        return w_ref[pl.ds(pl.multiple_of(group * TOK_GROUP, TOK_GROUP), TOK_GROUP), :]

    def broadcast_weights(wb_ref, wg, tl):
        d_hi, d_lo = _split_bf16(eye * wg[tl:tl + 1, :])
        wb_ref[tl * N_SEL:(tl + 1) * N_SEL, :] = (
            jnp.dot(d_hi, ones, preferred_element_type=jnp.float32)
            + jnp.dot(d_lo, ones, preferred_element_type=jnp.float32))

    def process(g, s, slot):
        wg_next = group_weights(jnp.minimum(g + 1, last_group))
        wb_ref = wb_refs[s]
        for tl in range(TOK_GROUP):
            broadcast_weights(wb_refs[1 - s], wg_next, tl)
            acc = acc_ref.at[tl * ROWS_PER_EXPERT:(tl + 1) * ROWS_PER_EXPERT, :]
            for kb in range(N_SEL // FOLD_BLOCK):
                part = None
                for k in range(kb * FOLD_BLOCK, (kb + 1) * FOLD_BLOCK):
                    row = tl * N_SEL + k
                    w = jnp.broadcast_to(wb_ref[row:row + 1, :], (ROWS_PER_EXPERT, LANES))
                    term = w * _gather_tile(tab_ref, slot[tl, k])
                    part = term if part is None else part + term
                acc[...] = part if kb == 0 else acc[...] + part
        tiles = [acc_ref[tl * ROWS_PER_EXPERT:(tl + 1) * ROWS_PER_EXPERT, :] for tl in range(TOK_GROUP)]
        base = pl.multiple_of(g * TOK_GROUP, TOK_GROUP)
        for r in range(ROWS_PER_EXPERT):
            out_ref[pl.ds(base, TOK_GROUP), r * LANES:(r + 1) * LANES] = jnp.concatenate(
                [tile[r:r + 1, :] for tile in tiles], axis=0)

    wg0 = group_weights(0)
    for tl in range(TOK_GROUP):
        broadcast_weights(wb0_ref, wg0, tl)
    _staged_groups(idx_ref, slots, sem, process)


def _peer_v(idx, w, tab):
    t = idx.shape[0]
    tb = min(TOK_BLOCK, t)
    sel = pl.BlockSpec((tb, N_SEL), lambda i: (i, 0))
    return pl.pallas_call(
        _peer_v_kernel,
        grid=(t // tb,),
        in_specs=[sel, sel, pl.BlockSpec(memory_space=pltpu.VMEM)],
        out_specs=pl.BlockSpec((tb, D_MODEL), lambda i: (i, 0)),
        out_shape=jax.ShapeDtypeStruct((t, D_MODEL), jnp.float32),
        scratch_shapes=_stage_scratch() + [pltpu.VMEM((TOK_GROUP * N_SEL, LANES), jnp.float32)] * 2
        + [pltpu.VMEM((TOK_GROUP * ROWS_PER_EXPERT, LANES), jnp.float32)],
        compiler_params=_cparams("arbitrary"),
        name="peer_v",
    )(idx, w, tab)


def _final_kernel(xp_ref, o_ref, g_ref, y_ref):
    y_ref[...] = _rms(xp_ref[...] + o_ref[...], g_ref[...])


def _final(xp, o, g):
    t = xp.shape[0]
    tm = min(512, t)
    tok = pl.BlockSpec((tm, D_MODEL), lambda i: (i, 0))
    return pl.pallas_call(
        _final_kernel,
        grid=(t // tm,),
        in_specs=[tok, tok, pl.BlockSpec((1, D_MODEL), lambda i: (0, 0))],
        out_specs=tok,
        out_shape=jax.ShapeDtypeStruct((t, D_MODEL), jnp.float32),
        compiler_params=_cparams("arbitrary"),
        name="final",
    )(xp, o, g)


def _rope_table(pos):
    inv = ROPE_THETA ** (-jnp.arange(0, QK_ROPE, 2, dtype=jnp.float32) / QK_ROPE)
    ang = pos.astype(jnp.float32)[:, None] * inv[None, :]
    c, s = jnp.cos(ang), jnp.sin(ang)
    return jnp.concatenate([c, c, -s, s], axis=-1)


def _swap_halves(w):
    half = w.shape[-1] // 2
    return jnp.concatenate([w[..., half:], w[..., :half]], axis=-1)


def _prepare_weights(g_mix_norm, w_in, g_q, w_uq, g_kv, w_uk, w_uv, conv_w, g_attn_out, g_conv_out,
                     w_o, g_ffn_norm, w_query, sub_keys, expert_u, expert_v):
    o1 = Q_RANK
    o2 = o1 + KV_RANK
    o3 = o2 + QK_ROPE
    w_kr = w_in[:, o2:o3]
    w_inx = jnp.concatenate([w_in[:, :o3], _swap_halves(w_kr), w_in[:, o3:]], axis=1)
    q_nope = w_uq[:, :, :QK_NOPE].reshape(Q_RANK, N_HEADS * QK_NOPE)
    q_rope = w_uq[:, :, QK_NOPE:]
    q_pair = jnp.concatenate([q_rope, _swap_halves(q_rope)], axis=-1).reshape(Q_RANK, N_HEADS * LANES)
    row = lambda g: g.reshape(1, -1)
    return {
        "g_mix": row(g_mix_norm), "w_in": _bf(w_inx), "g_q": row(g_q),
        "w_q": _bf(jnp.concatenate([q_nope, q_pair], axis=1)),
        "w_uk": _bf(jnp.transpose(w_uk, (1, 2, 0))), "g_kv": row(g_kv),
        "w_uv": _bf(jnp.transpose(w_uv, (1, 0, 2))), "conv_w": conv_w,
        "g_attn": row(g_attn_out), "g_conv": row(g_conv_out), "w_o": _bf(w_o),
        "g_ffn": row(g_ffn_norm), "w_query": _bf(w_query), "sub_keys": _bf(sub_keys),
        "tab_u": _pack_table(expert_u), "tab_v": _pack_table(expert_v),
    }


def _channel_mixer(attn, conv, x, wts, g_final):
    t = x.shape[0]
    xp, xn, scores = _mix(attn, conv, x, wts)
    idx_t, gate_t = _topk(scores)
    idx, gates = idx_t.T, gate_t.T
    w = _peer_u(idx, xn, gates, wts["tab_u"])
    o = _peer_v(idx, w, wts["tab_v"])
    return _final(xp, o, g_final.reshape(1, D_MODEL))


def kernel(x_prompt, x_sample, cache_ckv, cache_krope, state_conv, page_table, g_mix_norm, w_in, g_q, w_uq, g_kv, w_uk, w_uv, conv_w, g_attn_out, g_conv_out, w_o, g_ffn_norm, w_query, sub_keys, expert_u, expert_v, g_final):
    assert w_in.shape[0] == 1, "single-layer step"
    wts = _prepare_weights(g_mix_norm[0], w_in[0], g_q[0], w_uq[0], g_kv[0], w_uk[0], w_uv[0], conv_w[0],
                           g_attn_out[0], g_conv_out[0], w_o[0], g_ffn_norm[0], w_query[0], sub_keys[0],
                           expert_u[0], expert_v[0])
    b, s, _ = x_prompt.shape
    nb, n_pages = page_table.shape
    past = n_pages * PAGE_SIZE

    zero_prev = jnp.zeros((SUBLANES, CONV_WIDTH), jnp.float32)
    qcat, kcat, ckv_p, kr_p, conv_p, tail_p = _proj(
        x_prompt, _rope_table(jnp.arange(s)), zero_prev, zero_prev, wts, True)
    attn_p = _flash(qcat, kcat, wts)
    y_prompt = _channel_mixer(attn_p.reshape(b * s, ATTN_WIDTH), conv_p.reshape(b * s, CONV_WIDTH),
                              x_prompt.reshape(b * s, D_MODEL), wts, g_final).reshape(b, s, D_MODEL)
    new_conv_p = tail_p[:, SUBLANES - (CONV_K - 1):, :]

    xs = x_sample.reshape(1, nb, D_MODEL)
    cs_s = jnp.broadcast_to(_rope_table(jnp.full((1,), past)), (nb, LANES))
    s0, s1 = state_conv[0, :, 0, :], state_conv[0, :, 1, :]
    qcat_s, kcat_s, ckv_s, kr_s, conv_s, z_s = _proj(xs, cs_s, s0, s1, wts, False)
    q_dec = jnp.pad(jnp.transpose(qcat_s[0], (1, 0, 2)), ((0, 0), (0, Q_ROWS - N_HEADS), (0, 0)))
    o_lat = _decode(page_table, q_dec, kcat_s.reshape(nb, 1, QK_PAD), cache_ckv,
                    jnp.swapaxes(cache_krope, 2, 3))
    attn_s = _attn_out_call(jnp.transpose(o_lat[:, :N_HEADS], (1, 0, 2)), wts)
    y_sample = _channel_mixer(attn_s, conv_s[0], x_sample.reshape(nb, D_MODEL), wts, g_final)
    new_conv_s = jnp.stack([s1, z_s[0]], axis=1)

    return (y_prompt, y_sample.reshape(nb, 1, D_MODEL),
            ckv_p[None], kr_p[None], new_conv_p[None],
            ckv_s.reshape(1, nb, 1, KV_RANK), kr_s.reshape(1, nb, 1, QK_ROPE), new_conv_s[None])
```

```python
import functools

import jax
import jax.numpy as jnp
import numpy as np
from jax import lax
from jax.experimental import pallas as pl
from jax.experimental.pallas import tpu as pltpu

D_MODEL = 1024
N_HEADS = 4
QK_NOPE = 128
QK_ROPE = 64
V_DIM = 128
Q_RANK = 384
KV_RANK = 256
ATTN_WIDTH = N_HEADS * V_DIM
CONV_WIDTH = D_MODEL - ATTN_WIDTH
CONV_GROUPS = 4
CONV_K = 3
PEER_HEADS = 8
PEER_KEYS = 128
PEER_DK = 256
PEER_TOPK = 16
PAGE_SIZE = 128
ROPE_THETA = 10000.0
EPS = 1e-6
SM_SCALE = (QK_NOPE + QK_ROPE) ** -0.5

LANES = 128
SUBLANES = 8
VMEM_LIMIT = 48 * 1024 * 1024

QK_PAD = 384
N_SEL = PEER_HEADS * PEER_TOPK
ROWS_PER_EXPERT = D_MODEL // LANES
GROUPS = 2 * PEER_HEADS
D_INX = Q_RANK + KV_RANK + 2 * QK_ROPE + 3 * CONV_WIDTH
NEG_INF = float("-inf")

_NT = (((1,), (1,)), ((), ()))


def _cparams(*sem):
    return pltpu.CompilerParams(dimension_semantics=sem, vmem_limit_bytes=VMEM_LIMIT)


def _rms(x, g):
    return x * lax.rsqrt(jnp.mean(x * x, axis=-1, keepdims=True) + EPS) * g


def _bf(x):
    return x.astype(jnp.bfloat16)


def _rope_pair(t, cs):
    r = t * cs
    return r + pltpu.roll(r, QK_ROPE, axis=1)


def _proj_kernel(seq_conv, tm, x_ref, gmix_ref, win_ref, gq_ref, wq_ref, wuk_ref, gkv_ref,
                 convw_ref, cs_ref, s0_ref, s1_ref, gconv_ref,
                 qcat_ref, kcat_ref, ckv_ref, kr_ref, convn_ref, z_ref, zbuf_ref):
    si = pl.program_id(1)
    x = x_ref[0]
    xn = _bf(_rms(x, gmix_ref[...]))
    h = jnp.dot(xn, win_ref[...], preferred_element_type=jnp.float32)
    o1 = Q_RANK
    o2 = o1 + KV_RANK
    o3 = o2 + 2 * QK_ROPE
    o4 = o3 + CONV_WIDTH
    o5 = o4 + CONV_WIDTH
    cs = cs_ref[...]
    lane = lax.broadcasted_iota(jnp.int32, (tm, LANES), 1)
    rope_mask = lane < QK_ROPE

    ckv = _rms(h[:, o1:o2], gkv_ref[...])
    kr = _rope_pair(h[:, o2:o3], cs)
    ckv_ref[0] = ckv
    kr_ref[0] = kr[:, :QK_ROPE]
    kcat_ref[0, :, 0:KV_RANK] = _bf(ckv)
    kcat_ref[0, :, KV_RANK:QK_PAD] = _bf(jnp.where(rope_mask, kr, 0.0))

    cq = _bf(_rms(h[:, 0:o1], gq_ref[...]))
    q = jnp.dot(cq, wq_ref[...], preferred_element_type=jnp.float32)
    for hd in range(N_HEADS):
        qn = _bf(q[:, hd * QK_NOPE:(hd + 1) * QK_NOPE])
        qlat = jnp.dot(qn, wuk_ref[hd], preferred_element_type=jnp.float32)
        base = N_HEADS * QK_NOPE + hd * LANES
        qr = _rope_pair(q[:, base:base + LANES], cs)
        qcat_ref[0, hd, :, 0:KV_RANK] = _bf(qlat * SM_SCALE)
        qcat_ref[0, hd, :, KV_RANK:QK_PAD] = _bf(jnp.where(rope_mask, qr * SM_SCALE, 0.0))

    z = h[:, o4:o5] * h[:, o3:o4]
    zb = h[:, o5:]
    w0 = convw_ref[0:1, :]
    w1 = convw_ref[1:2, :]
    w2 = convw_ref[2:3, :]
    if seq_conv:
        @pl.when(si == 0)
        def _():
            zbuf_ref[0:SUBLANES, :] = jnp.zeros((SUBLANES, CONV_WIDTH), jnp.float32)

        zbuf_ref[SUBLANES:SUBLANES + tm, :] = z
        y = (w2 * z + w1 * zbuf_ref[SUBLANES - 1:SUBLANES - 1 + tm, :]
             + w0 * zbuf_ref[SUBLANES - 2:SUBLANES - 2 + tm, :])
        tail = zbuf_ref[tm:tm + SUBLANES, :]
        zbuf_ref[0:SUBLANES, :] = tail
        z_ref[0] = tail
    else:
        y = w2 * z + w1 * s1_ref[...] + w0 * s0_ref[...]
        z_ref[0] = z
    c = zb * y
    for g in range(CONV_GROUPS):
        sl = slice(g * LANES, (g + 1) * LANES)
        convn_ref[0, :, sl] = _bf(_rms(c[:, sl], gconv_ref[:, sl]))


def _proj(x, cs, s0, s1, wts, seq_conv):
    b, s, _ = x.shape
    tm = min(512, s)
    grid = (b, s // tm)
    z_rows = SUBLANES if seq_conv else tm
    full = lambda shape: pl.BlockSpec(shape, lambda i, j: (0,) * len(shape))
    tok = lambda width: pl.BlockSpec((1, tm, width), lambda i, j: (i, j, 0))
    prev = pl.BlockSpec((s0.shape[0] if seq_conv else tm, CONV_WIDTH),
                        lambda i, j: (0 if seq_conv else j, 0))
    out_shapes = (
        jax.ShapeDtypeStruct((b, N_HEADS, s, QK_PAD), jnp.bfloat16),
        jax.ShapeDtypeStruct((b, s, QK_PAD), jnp.bfloat16),
        jax.ShapeDtypeStruct((b, s, KV_RANK), jnp.float32),
        jax.ShapeDtypeStruct((b, s, QK_ROPE), jnp.float32),
        jax.ShapeDtypeStruct((b, s, CONV_WIDTH), jnp.bfloat16),
        jax.ShapeDtypeStruct((b, z_rows, CONV_WIDTH), jnp.float32),
    )
    out_specs = (
        pl.BlockSpec((1, N_HEADS, tm, QK_PAD), lambda i, j: (i, 0, j, 0)),
        tok(QK_PAD), tok(KV_RANK), tok(QK_ROPE), tok(CONV_WIDTH),
        pl.BlockSpec((1, z_rows, CONV_WIDTH), lambda i, j: (i, 0, 0)),
    )
    return pl.pallas_call(
        functools.partial(_proj_kernel, seq_conv, tm),
        grid=grid,
        in_specs=[
            tok(D_MODEL), full((1, D_MODEL)), full((D_MODEL, D_INX)), full((1, Q_RANK)),
            full((Q_RANK, 2 * N_HEADS * QK_NOPE)), full((N_HEADS, QK_NOPE, KV_RANK)),
            full((1, KV_RANK)), full((CONV_K, CONV_WIDTH)),
            pl.BlockSpec((tm, LANES), lambda i, j: (j, 0)), prev, prev,
            full((1, CONV_WIDTH)),
        ],
        out_specs=out_specs,
        out_shape=out_shapes,
        scratch_shapes=[pltpu.VMEM((tm + 2 * SUBLANES, CONV_WIDTH), jnp.float32)],
        compiler_params=_cparams("arbitrary", "arbitrary"),
        name="proj_seq" if seq_conv else "proj_tok",
    )(x, wts["g_mix"], wts["w_in"], wts["g_q"], wts["w_q"], wts["w_uk"], wts["g_kv"],
      wts["conv_w"], cs, s0, s1, wts["g_conv"])


def _attn_out(o, wuv_ref, gattn_ref, hd):
    a = jnp.dot(_bf(o), wuv_ref[hd], preferred_element_type=jnp.float32)
    return _bf(_rms(a, gattn_ref[:, hd * V_DIM:(hd + 1) * V_DIM]))


SOFTMAX_ROWS = 128


def _flash_kernel(tq, th, q_ref, k_ref, wuv_ref, gattn_ref, out_ref, m_ref, l_ref, acc_ref,
                  s0_ref, s1_ref, p0_ref, p1_ref, a0_ref, a1_ref):
    qi = pl.program_id(1)
    ki = pl.program_id(2)
    rows = N_HEADS * tq
    last_k = (qi * tq + tq - 1) // (2 * th)

    @pl.when(ki == 0)
    def _():
        m_ref[...] = jnp.full(m_ref.shape, NEG_INF, jnp.float32)
        l_ref[...] = jnp.zeros(l_ref.shape, jnp.float32)
        acc_ref[...] = jnp.zeros(acc_ref.shape, jnp.float32)

    def softmax_half(masked, half, s_ref, p_ref, a_ref):
        for c in range(rows // SOFTMAX_ROWS):
            rs = pl.ds(c * SOFTMAX_ROWS, SOFTMAX_ROWS)
            s = s_ref[rs, :]
            if masked:
                t0 = (c % (tq // SOFTMAX_ROWS)) * SOFTMAX_ROWS
                qpos = qi * tq + t0 + lax.broadcasted_iota(jnp.int32, (SOFTMAX_ROWS, th), 0)
                kpos = (2 * ki + half) * th + lax.broadcasted_iota(jnp.int32, (SOFTMAX_ROWS, th), 1)
                s = jnp.where(kpos <= qpos, s, NEG_INF)
            m_prev = m_ref[rs, :]
            m_new = jnp.maximum(m_prev, jnp.max(s, axis=-1, keepdims=True))
            alpha = jnp.exp(m_prev - m_new)
            p = jnp.exp(s - jnp.concatenate([m_new] * (th // LANES), axis=1))
            l_ref[rs, :] = alpha * l_ref[rs, :] + jnp.sum(p, axis=-1, keepdims=True)
            a_ref[rs, :] = alpha
            p_ref[rs, :] = _bf(p)
            m_ref[rs, :] = m_new

    def step(masked):
        q = q_ref[0].reshape(rows, QK_PAD)
        k0 = k_ref[0, 0:th, :]
        k1 = k_ref[0, th:2 * th, :]
        s0_ref[...] = lax.dot_general(q, k0, _NT, preferred_element_type=jnp.float32)
        s1_ref[...] = lax.dot_general(q, k1, _NT, preferred_element_type=jnp.float32)
        softmax_half(masked, 0, s0_ref, p0_ref, a0_ref)
        pv0 = jnp.dot(p0_ref[...], k0[:, 0:KV_RANK], preferred_element_type=jnp.float32)
        softmax_half(masked, 1, s1_ref, p1_ref, a1_ref)
        pv1 = jnp.dot(p1_ref[...], k1[:, 0:KV_RANK], preferred_element_type=jnp.float32)
        wide = lambda a: jnp.concatenate([a] * (KV_RANK // LANES), axis=1)
        acc_ref[...] = (acc_ref[...] * wide(a0_ref[...]) + pv0) * wide(a1_ref[...]) + pv1

    unmasked = (2 * ki + 2) * th <= qi * tq + 1
    pl.when(unmasked)(functools.partial(step, False))
    pl.when(jnp.logical_not(unmasked) & (ki <= last_k))(functools.partial(step, True))

    @pl.when(ki == last_k)
    def _():
        o = acc_ref[...] / l_ref[:, 0:1]
        for hd in range(N_HEADS):
            out_ref[0, :, hd * V_DIM:(hd + 1) * V_DIM] = _attn_out(
                o[hd * tq:(hd + 1) * tq], wuv_ref, gattn_ref, hd)


def _flash(qcat, kcat, wts):
    b, _, s, _ = qcat.shape
    tq, th = 512, 512
    tk = 2 * th
    rows = N_HEADS * tq
    return pl.pallas_call(
        functools.partial(_flash_kernel, tq, th),
        grid=(b, s // tq, s // tk),
        in_specs=[
            pl.BlockSpec((1, N_HEADS, tq, QK_PAD), lambda i, q, k: (i, 0, q, 0)),
            pl.BlockSpec((1, tk, QK_PAD),
                         lambda i, q, k: (i, jnp.minimum(k, (q * tq + tq - 1) // tk), 0)),
            pl.BlockSpec((N_HEADS, KV_RANK, V_DIM), lambda i, q, k: (0, 0, 0)),
            pl.BlockSpec((1, ATTN_WIDTH), lambda i, q, k: (0, 0)),
        ],
        out_specs=pl.BlockSpec((1, tq, ATTN_WIDTH), lambda i, q, k: (i, q, 0)),
        out_shape=jax.ShapeDtypeStruct((b, s, ATTN_WIDTH), jnp.bfloat16),
        scratch_shapes=[pltpu.VMEM((rows, LANES), jnp.float32),
                        pltpu.VMEM((rows, LANES), jnp.float32),
                        pltpu.VMEM((rows, KV_RANK), jnp.float32),
                        pltpu.VMEM((rows, th), jnp.float32), pltpu.VMEM((rows, th), jnp.float32),
                        pltpu.VMEM((rows, th), jnp.bfloat16), pltpu.VMEM((rows, th), jnp.bfloat16),
                        pltpu.VMEM((rows, LANES), jnp.float32), pltpu.VMEM((rows, LANES), jnp.float32)],
        compiler_params=_cparams("arbitrary", "arbitrary", "arbitrary"),
        name="flash",
    )(qcat, kcat, wts["w_uv"], wts["g_attn"])


PAGES_PER_STEP = 64
Q_ROWS = 16


def _decode_kernel(n_steps, pt_ref, q_ref, knew_ref, ckv_hbm, kr_hbm, out_ref,
                   ckv_buf, kr_buf, sem, m_ref, l_ref, acc_ref):
    b = pl.program_id(0)
    j = pl.program_id(1)
    total_steps = pl.num_programs(0) * n_steps
    step = b * n_steps + j
    slot = step % 2

    def page_copies(page_of, sl):
        copies = []
        for i in range(PAGES_PER_STEP):
            page = page_of(i)
            copies.append(pltpu.make_async_copy(ckv_hbm.at[0, page], ckv_buf.at[sl, i], sem.at[0, sl]))
            copies.append(pltpu.make_async_copy(kr_hbm.at[0, page], kr_buf.at[sl, i], sem.at[1, sl]))
        return copies

    @pl.when(step == 0)
    def _():
        for c in page_copies(lambda i: pt_ref[0, i], 0):
            c.start()

    nxt = step + 1

    @pl.when(nxt < total_steps)
    def _():
        nb_, nj_ = nxt // n_steps, nxt % n_steps
        for c in page_copies(lambda i: pt_ref[nb_, nj_ * PAGES_PER_STEP + i], 1 - slot):
            c.start()

    for c in page_copies(lambda i: 0, slot):
        c.wait()

    @pl.when(j == 0)
    def _():
        m_ref[...] = jnp.full(m_ref.shape, NEG_INF, jnp.float32)
        l_ref[...] = jnp.zeros(l_ref.shape, jnp.float32)
        acc_ref[...] = jnp.zeros(acc_ref.shape, jnp.float32)

    q = q_ref[0]
    qlat = q[:, 0:KV_RANK]
    qrope = q[:, KV_RANK:KV_RANK + QK_ROPE]
    vals = [_bf(ckv_buf[slot, i]) for i in range(PAGES_PER_STEP)]
    s = jnp.concatenate(
        [lax.dot_general(qlat, v, _NT, preferred_element_type=jnp.float32)
         + jnp.dot(qrope, _bf(kr_buf[slot, i]), preferred_element_type=jnp.float32)
         for i, v in enumerate(vals)], axis=-1)
    m_prev = m_ref[...]
    m_new = jnp.maximum(m_prev, jnp.max(s, axis=-1, keepdims=True))
    alpha = jnp.exp(m_prev - m_new)
    p = jnp.exp(s - m_new[:, 0:1])
    l_new = alpha * l_ref[...] + jnp.sum(p, axis=-1, keepdims=True)
    acc = acc_ref[...] * alpha[:, 0:1]
    for i, v in enumerate(vals):
        acc = acc + jnp.dot(_bf(p[:, i * PAGE_SIZE:(i + 1) * PAGE_SIZE]), v,
                            preferred_element_type=jnp.float32)
    m_ref[...] = m_new
    l_ref[...] = l_new
    acc_ref[...] = acc

    @pl.when(j == n_steps - 1)
    def _():
        kn = knew_ref[0].astype(jnp.float32)
        s_new = jnp.sum(q.astype(jnp.float32) * kn, axis=-1, keepdims=True)
        m_fin = jnp.maximum(m_new, s_new)
        a2 = jnp.exp(m_new - m_fin)
        p_new = jnp.exp(s_new - m_fin[:, 0:1])
        l_fin = a2 * l_new + p_new
        acc_fin = acc * a2[:, 0:1] + _bf(p_new).astype(jnp.float32) * kn[:, 0:KV_RANK]
        out_ref[0] = acc_fin / l_fin[:, 0:1]


def _decode(page_table, q, knew, cache_ckv, cache_krope_t):
    nb, n_pages = page_table.shape
    n_steps = n_pages // PAGES_PER_STEP

    grid_spec = pltpu.PrefetchScalarGridSpec(
        num_scalar_prefetch=1,
        grid=(nb, n_steps),
        in_specs=[pl.BlockSpec((1, Q_ROWS, QK_PAD), lambda b, j, pt: (b, 0, 0)),
                  pl.BlockSpec((1, 1, QK_PAD), lambda b, j, pt: (b, 0, 0)),
                  pl.BlockSpec(memory_space=pl.ANY), pl.BlockSpec(memory_space=pl.ANY)],
        out_specs=pl.BlockSpec((1, Q_ROWS, KV_RANK), lambda b, j, pt: (b, 0, 0)),
        scratch_shapes=[pltpu.VMEM((2, PAGES_PER_STEP, PAGE_SIZE, KV_RANK), jnp.float32),
                        pltpu.VMEM((2, PAGES_PER_STEP, QK_ROPE, PAGE_SIZE), jnp.float32),
                        pltpu.SemaphoreType.DMA((2, 2)),
                        pltpu.VMEM((Q_ROWS, LANES), jnp.float32),
                        pltpu.VMEM((Q_ROWS, LANES), jnp.float32),
                        pltpu.VMEM((Q_ROWS, KV_RANK), jnp.float32)],
    )
    return pl.pallas_call(
        functools.partial(_decode_kernel, n_steps),
        grid_spec=grid_spec,
        out_shape=jax.ShapeDtypeStruct((nb, Q_ROWS, KV_RANK), jnp.float32),
        compiler_params=_cparams("arbitrary", "arbitrary"),
        name="decode",
    )(page_table, q, knew, cache_ckv, cache_krope_t)


def _attn_out_kernel(o_ref, wuv_ref, gattn_ref, out_ref):
    for hd in range(N_HEADS):
        out_ref[:, hd * V_DIM:(hd + 1) * V_DIM] = _attn_out(o_ref[hd], wuv_ref, gattn_ref, hd)


def _attn_out_call(o, wts):
    t = o.shape[1]
    return pl.pallas_call(
        _attn_out_kernel,
        out_shape=jax.ShapeDtypeStruct((t, ATTN_WIDTH), jnp.bfloat16),
        name="attn_out",
    )(o, wts["w_uv"], wts["g_attn"])


def _mix_kernel(attn_ref, conv_ref, x_ref, wo_ref, gffn_ref, wqry_ref, keys_ref,
                xp_ref, xn_ref, sc_ref):
    y = (jnp.dot(attn_ref[...], wo_ref[0:ATTN_WIDTH, :], preferred_element_type=jnp.float32)
         + jnp.dot(conv_ref[...], wo_ref[ATTN_WIDTH:, :], preferred_element_type=jnp.float32))
    xp = x_ref[...] + y
    xp_ref[...] = xp
    xn = _rms(xp, gffn_ref[...])
    xn_ref[...] = xn
    q = jnp.dot(_bf(xn), wqry_ref[...], preferred_element_type=jnp.float32)
    half = PEER_DK // 2
    for g in range(GROUPS):
        sc_ref[g] = lax.dot_general(keys_ref[g % 2], _bf(q[:, g * half:(g + 1) * half]), _NT,
                                    preferred_element_type=jnp.float32)


def _mix(attn, conv, x, wts):
    t = x.shape[0]
    tm = min(256, t)
    full = lambda shape: pl.BlockSpec(shape, lambda i: (0,) * len(shape))
    tok = lambda width: pl.BlockSpec((tm, width), lambda i: (i, 0))
    return pl.pallas_call(
        _mix_kernel,
        grid=(t // tm,),
        in_specs=[tok(ATTN_WIDTH), tok(CONV_WIDTH), tok(D_MODEL), full((D_MODEL, D_MODEL)),
                  full((1, D_MODEL)), full((D_MODEL, PEER_HEADS * PEER_DK)),
                  full((2, PEER_KEYS, PEER_DK // 2))],
        out_specs=(tok(D_MODEL), tok(D_MODEL),
                   pl.BlockSpec((GROUPS, PEER_KEYS, tm), lambda i: (0, 0, i))),
        out_shape=(jax.ShapeDtypeStruct((t, D_MODEL), jnp.float32),
                   jax.ShapeDtypeStruct((t, D_MODEL), jnp.float32),
                   jax.ShapeDtypeStruct((GROUPS, PEER_KEYS, t), jnp.float32)),
        compiler_params=_cparams("arbitrary"),
        name="mix",
    )(attn, conv, x, wts["w_o"], wts["g_ffn"], wts["w_query"], wts["sub_keys"])


def _top_rounds(s, ids, payload, n):
    vals, idxs, pays = [], [], []
    sentinel = jnp.int32(2 ** 30)
    for r in range(n):
        m = jnp.max(s, axis=0, keepdims=True)
        win = jnp.min(jnp.where(s == m, ids, sentinel), axis=0, keepdims=True)
        vals.append(m)
        idxs.append(win)
        if payload is not None:
            pays.append(jnp.max(jnp.where(ids == win, payload, -1), axis=0, keepdims=True))
        if r < n - 1:
            s = jnp.where(ids == win, NEG_INF, s)
    cat = lambda xs: jnp.concatenate(xs, axis=0)
    return cat(vals), cat(idxs), (cat(pays) if payload is not None else None)


def _topk_kernel(tt, sc_ref, idx_ref, gate_ref):
    k = PEER_TOPK
    key_ids = lax.broadcasted_iota(jnp.int32, (PEER_KEYS, tt), 0)
    blocks = [(a, a + 1, 0, k // (a + 1)) for a in range(k // 2)] + [(k // 2, k, 0, 1)]
    n_cand = sum((a1 - a0) * (b1 - b0) for a0, a1, b0, b1 in blocks)
    n_pad = -n_cand % SUBLANES
    for hd in range(PEER_HEADS):
        v0, i0, _ = _top_rounds(sc_ref[2 * hd], key_ids, None, k)
        v1, i1, _ = _top_rounds(sc_ref[2 * hd + 1], key_ids, None, k)
        cand, cid, eid = [], [], []
        for (a0, a1, b0, b1) in blocks:
            na, nb = a1 - a0, b1 - b0
            if na == 1:
                va, ia, vb, ib = v0[a0:a1], i0[a0:a1], v1[b0:b1], i1[b0:b1]
                rank = a0 * k + b0 + lax.broadcasted_iota(jnp.int32, (nb, tt), 0)
            else:
                va, ia, vb, ib = v0[a0:a1], i0[a0:a1], v1[b0:b1], i1[b0:b1]
                rank = (a0 + lax.broadcasted_iota(jnp.int32, (na, tt), 0)) * k + b0
            cand.append(va + vb)
            eid.append(ia * PEER_KEYS + ib)
            cid.append(rank)
        if n_pad:
            cand.append(jnp.full((n_pad, tt), NEG_INF, jnp.float32))
            eid.append(jnp.zeros((n_pad, tt), jnp.int32))
            cid.append(jnp.full((n_pad, tt), k * k, jnp.int32))
        cand = jnp.concatenate(cand, axis=0)
        cid = jnp.concatenate(cid, axis=0)
        eid = jnp.concatenate(eid, axis=0)
        ts, _, te = _top_rounds(cand, cid, eid, k)
        e = jnp.exp(ts - ts[0:1])
        g = e / jnp.sum(e, axis=0, keepdims=True)
        idx_ref[hd * k:(hd + 1) * k, :] = te * PACKED_ROWS
        gate_ref[hd * k:(hd + 1) * k, :] = g


def _topk(scores):
    t = scores.shape[2]
    tt = LANES
    return pl.pallas_call(
        functools.partial(_topk_kernel, tt),
        grid=(t // tt,),
        in_specs=[pl.BlockSpec((GROUPS, PEER_KEYS, tt), lambda i: (0, 0, i))],
        out_specs=(pl.BlockSpec((N_SEL, tt), lambda i: (0, i)),
                   pl.BlockSpec((N_SEL, tt), lambda i: (0, i))),
        out_shape=(jax.ShapeDtypeStruct((N_SEL, t), jnp.int32),
                   jax.ShapeDtypeStruct((N_SEL, t), jnp.float32)),
        compiler_params=_cparams("arbitrary"),
        name="topk",
    )(scores)


TOK_BLOCK = 128
TOK_GROUP = 8
N_SLOTS = 4
PACKED_ROWS = ROWS_PER_EXPERT // 2
PACK_BLOCK = 8192


def _split_bf16(x):
    hi = _bf(x)
    lo = _bf(x - hi.astype(jnp.float32))
    return hi, lo


def _pack_kernel(x_ref, o_ref):
    o_ref[...] = pltpu.bitcast(_bf(x_ref[...]), jnp.int32)


def _pack_table(tab):
    rows = tab.size // LANES
    return pl.pallas_call(
        _pack_kernel,
        grid=(rows // PACK_BLOCK,),
        in_specs=[pl.BlockSpec((PACK_BLOCK, LANES), lambda i: (i, 0))],
        out_specs=pl.BlockSpec((PACK_BLOCK // 2, LANES), lambda i: (i, 0)),
        out_shape=jax.ShapeDtypeStruct((rows // 2, LANES), jnp.int32),
        compiler_params=_cparams("arbitrary"),
        name="pack_table",
    )(tab.reshape(rows, LANES))


def _gather_tile(tab_ref, off):
    w = tab_ref[pl.ds(pl.multiple_of(off, PACKED_ROWS), PACKED_ROWS), :]
    return pltpu.bitcast(w, jnp.bfloat16).astype(jnp.float32)


def _token_tile(xg, tl):
    return jnp.concatenate(
        [xg[tl:tl + 1, r * LANES:(r + 1) * LANES] for r in range(ROWS_PER_EXPERT)], axis=0)


def _staged_groups(idx_ref, slots, sem, process):
    n_slots = len(slots)
    n_groups = idx_ref.shape[0] // TOK_GROUP
    assert n_slots % 2 == 0 and n_groups % n_slots == 0

    def copy(g, s):
        rows = pl.ds(pl.multiple_of(g * TOK_GROUP, TOK_GROUP), TOK_GROUP)
        return pltpu.make_async_copy(idx_ref.at[rows, :], slots[s], sem.at[s])

    for s in range(n_slots - 1):
        copy(s, s).start()

    def ring(i, carry):
        for s in range(n_slots):
            g = n_slots * i + s
            copy(g, s).wait()
            ahead = n_slots - 1

            @pl.when(g + ahead < n_groups)
            def _():
                copy(g + ahead, (s + ahead) % n_slots).start()

            process(g, s % 2, slots[s])
        return carry

    lax.fori_loop(0, n_groups // n_slots, ring, 0)
    return n_groups


FOLD_BLOCK = SUBLANES
P_PITCH = ROWS_PER_EXPERT + 1


def _peer_u_kernel(idx_ref, x_ref, gate_ref, tab_ref, out_ref, *scratch):
    slots, sem = scratch[:N_SLOTS], scratch[N_SLOTS]
    f0_ref, f1_ref = scratch[N_SLOTS + 1:N_SLOTS + 3]
    p_refs = scratch[N_SLOTS + 3:]
    ones = jnp.ones((SUBLANES, LANES), jnp.bfloat16)
    sub = lax.broadcasted_iota(jnp.int32, (SUBLANES, LANES), 0)
    f_refs = (f0_ref, f1_ref)

    def lane_fold(f_ref, tl, rows):
        f_hi, f_lo = _split_bf16(f_ref[tl * N_SEL:(tl + 1) * N_SEL, :])
        a = (lax.dot_general(ones, f_hi, _NT, preferred_element_type=jnp.float32)
             + lax.dot_general(ones, f_lo, _NT, preferred_element_type=jnp.float32))
        return a if rows is None else jnp.where(sub == tl, a, rows)

    def store_rows(group, rows):
        out_ref[pl.ds(pl.multiple_of(group * TOK_GROUP, TOK_GROUP), TOK_GROUP), :] = rows

    def process(g, s, slot):
        xg = x_ref[pl.ds(pl.multiple_of(g * TOK_GROUP, TOK_GROUP), TOK_GROUP), :]
        rows = None
        for tl in range(TOK_GROUP + 1):
            if tl < TOK_GROUP:
                xt = _token_tile(xg, tl)
                rows = lane_fold(f_refs[1 - s], tl, rows)
            for kb in range(N_SEL // FOLD_BLOCK):
                if tl < TOK_GROUP:
                    for k in range(kb * FOLD_BLOCK, (kb + 1) * FOLD_BLOCK):
                        p_refs[tl][k * P_PITCH:k * P_PITCH + ROWS_PER_EXPERT, :] = (
                            _gather_tile(tab_ref, slot[tl, k]) * xt)
                if tl >= 1:
                    p_ref = p_refs[tl - 1]
                    first = kb * FOLD_BLOCK * P_PITCH
                    parts = [p_ref[pl.ds(first + r, FOLD_BLOCK, stride=P_PITCH), :]
                             for r in range(ROWS_PER_EXPERT)]
                    while len(parts) > 1:
                        parts = [a + b for a, b in zip(parts[0::2], parts[1::2])]
                    row = (tl - 1) * N_SEL + kb * FOLD_BLOCK
                    f_refs[s][row:row + FOLD_BLOCK, :] = parts[0]
        store_rows(jnp.maximum(g - 1, 0), rows)

    f1_ref[...] = jnp.zeros(f1_ref.shape, jnp.float32)
    n_groups = _staged_groups(idx_ref, slots, sem, process)
    rows = None
    for tl in range(TOK_GROUP):
        rows = lane_fold(f_refs[(n_groups - 1) % 2], tl, rows)
    store_rows(n_groups - 1, rows)
    a = out_ref[...]
    out_ref[...] = gate_ref[...] * (0.5 * a * (1.0 + lax.erf(a * (2.0 ** -0.5))))


def _stage_scratch():
    return [pltpu.SMEM((TOK_GROUP, N_SEL), jnp.int32)] * N_SLOTS + [pltpu.SemaphoreType.DMA((N_SLOTS,))]


def _peer_u(idx, xn, gates, tab):
    t = idx.shape[0]
    tb = min(TOK_BLOCK, t)
    sel = pl.BlockSpec((tb, N_SEL), lambda i: (i, 0))
    return pl.pallas_call(
        _peer_u_kernel,
        grid=(t // tb,),
        in_specs=[sel, pl.BlockSpec((tb, D_MODEL), lambda i: (i, 0)), sel,
                  pl.BlockSpec(memory_space=pltpu.VMEM)],
        out_specs=sel,
        out_shape=jax.ShapeDtypeStruct((t, N_SEL), jnp.float32),
        scratch_shapes=_stage_scratch()
        + [pltpu.VMEM((TOK_GROUP * N_SEL, LANES), jnp.float32)] * 2
        + [pltpu.VMEM((N_SEL * P_PITCH, LANES), jnp.float32)] * TOK_GROUP,
        compiler_params=_cparams("arbitrary"),
        name="peer_u",
    )(idx, xn, gates, tab)


def _peer_v_kernel(idx_ref, w_ref, tab_ref, out_ref, *scratch):
    slots, sem = scratch[:N_SLOTS], scratch[N_SLOTS]
    wb0_ref, wb1_ref, acc_ref = scratch[N_SLOTS + 1:]
    ones = jnp.ones((LANES, LANES), jnp.bfloat16)
    eye = (lax.broadcasted_iota(jnp.int32, (N_SEL, LANES), 0)
           == lax.broadcasted_iota(jnp.int32, (N_SEL, LANES), 1)).astype(jnp.float32)
    wb_refs = (wb0_ref, wb1_ref)
    last_group = idx_ref.shape[0] // TOK_GROUP - 1

    def group_weights(group):
        return w_ref[pl.ds(pl.multiple_of(group * TOK_GROUP, TOK_GROUP), TOK_GROUP), :]

    def broadcast_weights(wb_ref, wg, tl):
        d_hi, d_lo = _split_bf16(eye * wg[tl:tl + 1, :])
        wb_ref[tl * N_SEL:(tl + 1) * N_SEL, :] = (
            jnp.dot(d_hi, ones, preferred_element_type=jnp.float32)
            + jnp.dot(d_lo, ones, preferred_element_type=jnp.float32))

    def process(g, s, slot):
        wg_next = group_weights(jnp.minimum(g + 1, last_group))
        wb_ref = wb_refs[s]
        for tl in range(TOK_GROUP):
            broadcast_weights(wb_refs[1 - s], wg_next, tl)
            acc = acc_ref.at[tl * ROWS_PER_EXPERT:(tl + 1) * ROWS_PER_EXPERT, :]
            for kb in range(N_SEL // FOLD_BLOCK):
                part = None
                for k in range(kb * FOLD_BLOCK, (kb + 1) * FOLD_BLOCK):
                    row = tl * N_SEL + k
                    w = jnp.broadcast_to(wb_ref[row:row + 1, :], (ROWS_PER_EXPERT, LANES))
                    term = w * _gather_tile(tab_ref, slot[tl, k])
                    part = term if part is None else part + term
                acc[...] = part if kb == 0 else acc[...] + part
        tiles = [acc_ref[tl * ROWS_PER_EXPERT:(tl + 1) * ROWS_PER_EXPERT, :] for tl in range(TOK_GROUP)]
        base = pl.multiple_of(g * TOK_GROUP, TOK_GROUP)
        for r in range(ROWS_PER_EXPERT):
            out_ref[pl.ds(base, TOK_GROUP), r * LANES:(r + 1) * LANES] = jnp.concatenate(
                [tile[r:r + 1, :] for tile in tiles], axis=0)

    wg0 = group_weights(0)
    for tl in range(TOK_GROUP):
        broadcast_weights(wb0_ref, wg0, tl)
    _staged_groups(idx_ref, slots, sem, process)


def _peer_v(idx, w, tab):
    t = idx.shape[0]
    tb = min(TOK_BLOCK, t)
    sel = pl.BlockSpec((tb, N_SEL), lambda i: (i, 0))
    return pl.pallas_call(
        _peer_v_kernel,
        grid=(t // tb,),
        in_specs=[sel, sel, pl.BlockSpec(memory_space=pltpu.VMEM)],
        out_specs=pl.BlockSpec((tb, D_MODEL), lambda i: (i, 0)),
        out_shape=jax.ShapeDtypeStruct((t, D_MODEL), jnp.float32),
        scratch_shapes=_stage_scratch() + [pltpu.VMEM((TOK_GROUP * N_SEL, LANES), jnp.float32)] * 2
        + [pltpu.VMEM((TOK_GROUP * ROWS_PER_EXPERT, LANES), jnp.float32)],
        compiler_params=_cparams("arbitrary"),
        name="peer_v",
    )(idx, w, tab)


def _final_kernel(xp_ref, o_ref, g_ref, y_ref):
    y_ref[...] = _rms(xp_ref[...] + o_ref[...], g_ref[...])


def _final(xp, o, g):
    t = xp.shape[0]
    tm = min(512, t)
    tok = pl.BlockSpec((tm, D_MODEL), lambda i: (i, 0))
    return pl.pallas_call(
        _final_kernel,
        grid=(t // tm,),
        in_specs=[tok, tok, pl.BlockSpec((1, D_MODEL), lambda i: (0, 0))],
        out_specs=tok,
        out_shape=jax.ShapeDtypeStruct((t, D_MODEL), jnp.float32),
        compiler_params=_cparams("arbitrary"),
        name="final",
    )(xp, o, g)


def _rope_table(pos):
    inv = ROPE_THETA ** (-jnp.arange(0, QK_ROPE, 2, dtype=jnp.float32) / QK_ROPE)
    ang = pos.astype(jnp.float32)[:, None] * inv[None, :]
    c, s = jnp.cos(ang), jnp.sin(ang)
    return jnp.concatenate([c, c, -s, s], axis=-1)


def _swap_halves(w):
    half = w.shape[-1] // 2
    return jnp.concatenate([w[..., half:], w[..., :half]], axis=-1)


def _prepare_weights(g_mix_norm, w_in, g_q, w_uq, g_kv, w_uk, w_uv, conv_w, g_attn_out, g_conv_out,
                     w_o, g_ffn_norm, w_query, sub_keys, expert_u, expert_v):
    o1 = Q_RANK
    o2 = o1 + KV_RANK
    o3 = o2 + QK_ROPE
    w_kr = w_in[:, o2:o3]
    w_inx = jnp.concatenate([w_in[:, :o3], _swap_halves(w_kr), w_in[:, o3:]], axis=1)
    q_nope = w_uq[:, :, :QK_NOPE].reshape(Q_RANK, N_HEADS * QK_NOPE)
    q_rope = w_uq[:, :, QK_NOPE:]
    q_pair = jnp.concatenate([q_rope, _swap_halves(q_rope)], axis=-1).reshape(Q_RANK, N_HEADS * LANES)
    row = lambda g: g.reshape(1, -1)
    return {
        "g_mix": row(g_mix_norm), "w_in": _bf(w_inx), "g_q": row(g_q),
        "w_q": _bf(jnp.concatenate([q_nope, q_pair], axis=1)),
        "w_uk": _bf(jnp.transpose(w_uk, (1, 2, 0))), "g_kv": row(g_kv),
        "w_uv": _bf(jnp.transpose(w_uv, (1, 0, 2))), "conv_w": conv_w,
        "g_attn": row(g_attn_out), "g_conv": row(g_conv_out), "w_o": _bf(w_o),
        "g_ffn": row(g_ffn_norm), "w_query": _bf(w_query), "sub_keys": _bf(sub_keys),
        "tab_u": _pack_table(expert_u), "tab_v": _pack_table(expert_v),
    }


def _channel_mixer(attn, conv, x, wts, g_final):
    t = x.shape[0]
    xp, xn, scores = _mix(attn, conv, x, wts)
    idx_t, gate_t = _topk(scores)
    idx, gates = idx_t.T, gate_t.T
    w = _peer_u(idx, xn, gates, wts["tab_u"])
    o = _peer_v(idx, w, wts["tab_v"])
    return _final(xp, o, g_final.reshape(1, D_MODEL))


def kernel(x_prompt, x_sample, cache_ckv, cache_krope, state_conv, page_table, g_mix_norm, w_in, g_q, w_uq, g_kv, w_uk, w_uv, conv_w, g_attn_out, g_conv_out, w_o, g_ffn_norm, w_query, sub_keys, expert_u, expert_v, g_final):
    assert w_in.shape[0] == 1, "single-layer step"
    wts = _prepare_weights(g_mix_norm[0], w_in[0], g_q[0], w_uq[0], g_kv[0], w_uk[0], w_uv[0], conv_w[0],
                           g_attn_out[0], g_conv_out[0], w_o[0], g_ffn_norm[0], w_query[0], sub_keys[0],
                           expert_u[0], expert_v[0])
    b, s, _ = x_prompt.shape
    nb, n_pages = page_table.shape
    past = n_pages * PAGE_SIZE

    zero_prev = jnp.zeros((SUBLANES, CONV_WIDTH), jnp.float32)
    qcat, kcat, ckv_p, kr_p, conv_p, tail_p = _proj(
        x_prompt, _rope_table(jnp.arange(s)), zero_prev, zero_prev, wts, True)
    attn_p = _flash(qcat, kcat, wts)
    y_prompt = _channel_mixer(attn_p.reshape(b * s, ATTN_WIDTH), conv_p.reshape(b * s, CONV_WIDTH),
                              x_prompt.reshape(b * s, D_MODEL), wts, g_final).reshape(b, s, D_MODEL)
    new_conv_p = tail_p[:, SUBLANES - (CONV_K - 1):, :]

    xs = x_sample.reshape(1, nb, D_MODEL)
    cs_s = jnp.broadcast_to(_rope_table(jnp.full((1,), past)), (nb, LANES))
    s0, s1 = state_conv[0, :, 0, :], state_conv[0, :, 1, :]
    qcat_s, kcat_s, ckv_s, kr_s, conv_s, z_s = _proj(xs, cs_s, s0, s1, wts, False)
    q_dec = jnp.pad(jnp.transpose(qcat_s[0], (1, 0, 2)), ((0, 0), (0, Q_ROWS - N_HEADS), (0, 0)))
    o_lat = _decode(page_table, q_dec, kcat_s.reshape(nb, 1, QK_PAD), cache_ckv,
                    jnp.swapaxes(cache_krope, 2, 3))
    attn_s = _attn_out_call(jnp.transpose(o_lat[:, :N_HEADS], (1, 0, 2)), wts)
    y_sample = _channel_mixer(attn_s, conv_s[0], x_sample.reshape(nb, D_MODEL), wts, g_final)
    new_conv_s = jnp.stack([s1, z_s[0]], axis=1)

    return (y_prompt, y_sample.reshape(nb, 1, D_MODEL),
            ckv_p[None], kr_p[None], new_conv_p[None],
            ckv_s.reshape(1, nb, 1, KV_RANK), kr_s.reshape(1, nb, 1, QK_ROPE), new_conv_s[None])
```

```python
import functools

import jax
import jax.numpy as jnp
import numpy as np
from jax import lax
from jax.experimental import pallas as pl
from jax.experimental.pallas import tpu as pltpu

D_MODEL = 1024
N_HEADS = 4
QK_NOPE = 128
QK_ROPE = 64
V_DIM = 128
Q_RANK = 384
KV_RANK = 256
ATTN_WIDTH = N_HEADS * V_DIM
CONV_WIDTH = D_MODEL - ATTN_WIDTH
CONV_GROUPS = 4
CONV_K = 3
PEER_HEADS = 8
PEER_KEYS = 128
PEER_DK = 256
PEER_TOPK = 16
PAGE_SIZE = 128
ROPE_THETA = 10000.0
EPS = 1e-6
SM_SCALE = (QK_NOPE + QK_ROPE) ** -0.5

LANES = 128
SUBLANES = 8
VMEM_LIMIT = 48 * 1024 * 1024

QK_PAD = 384
N_SEL = PEER_HEADS * PEER_TOPK
ROWS_PER_EXPERT = D_MODEL // LANES
GROUPS = 2 * PEER_HEADS
D_INX = Q_RANK + KV_RANK + 2 * QK_ROPE + 3 * CONV_WIDTH
NEG_INF = float("-inf")

_NT = (((1,), (1,)), ((), ()))


def _cparams(*sem):
    return pltpu.CompilerParams(dimension_semantics=sem, vmem_limit_bytes=VMEM_LIMIT)


def _rms(x, g):
    return x * lax.rsqrt(jnp.mean(x * x, axis=-1, keepdims=True) + EPS) * g


def _bf(x):
    return x.astype(jnp.bfloat16)


def _rope_pair(t, cs):
    r = t * cs
    return r + pltpu.roll(r, QK_ROPE, axis=1)


def _proj_kernel(seq_conv, tm, x_ref, gmix_ref, win_ref, gq_ref, wq_ref, wuk_ref, gkv_ref,
                 convw_ref, cs_ref, s0_ref, s1_ref, gconv_ref,
                 qcat_ref, kcat_ref, ckv_ref, kr_ref, convn_ref, z_ref, zbuf_ref):
    si = pl.program_id(1)
    x = x_ref[0]
    xn = _bf(_rms(x, gmix_ref[...]))
    h = jnp.dot(xn, win_ref[...], preferred_element_type=jnp.float32)
    o1 = Q_RANK
    o2 = o1 + KV_RANK
    o3 = o2 + 2 * QK_ROPE
    o4 = o3 + CONV_WIDTH
    o5 = o4 + CONV_WIDTH
    cs = cs_ref[...]
    lane = lax.broadcasted_iota(jnp.int32, (tm, LANES), 1)
    rope_mask = lane < QK_ROPE

    ckv = _rms(h[:, o1:o2], gkv_ref[...])
    kr = _rope_pair(h[:, o2:o3], cs)
    ckv_ref[0] = ckv
    kr_ref[0] = kr[:, :QK_ROPE]
    kcat_ref[0, :, 0:KV_RANK] = _bf(ckv)
    kcat_ref[0, :, KV_RANK:QK_PAD] = _bf(jnp.where(rope_mask, kr, 0.0))

    cq = _bf(_rms(h[:, 0:o1], gq_ref[...]))
    q = jnp.dot(cq, wq_ref[...], preferred_element_type=jnp.float32)
    for hd in range(N_HEADS):
        qn = _bf(q[:, hd * QK_NOPE:(hd + 1) * QK_NOPE])
        qlat = jnp.dot(qn, wuk_ref[hd], preferred_element_type=jnp.float32)
        base = N_HEADS * QK_NOPE + hd * LANES
        qr = _rope_pair(q[:, base:base + LANES], cs)
        qcat_ref[0, hd, :, 0:KV_RANK] = _bf(qlat * SM_SCALE)
        qcat_ref[0, hd, :, KV_RANK:QK_PAD] = _bf(jnp.where(rope_mask, qr * SM_SCALE, 0.0))

    z = h[:, o4:o5] * h[:, o3:o4]
    zb = h[:, o5:]
    w0 = convw_ref[0:1, :]
    w1 = convw_ref[1:2, :]
    w2 = convw_ref[2:3, :]
    if seq_conv:
        @pl.when(si == 0)
        def _():
            zbuf_ref[0:SUBLANES, :] = jnp.zeros((SUBLANES, CONV_WIDTH), jnp.float32)

        zbuf_ref[SUBLANES:SUBLANES + tm, :] = z
        y = (w2 * z + w1 * zbuf_ref[SUBLANES - 1:SUBLANES - 1 + tm, :]
             + w0 * zbuf_ref[SUBLANES - 2:SUBLANES - 2 + tm, :])
        tail = zbuf_ref[tm:tm + SUBLANES, :]
        zbuf_ref[0:SUBLANES, :] = tail
        z_ref[0] = tail
    else:
        y = w2 * z + w1 * s1_ref[...] + w0 * s0_ref[...]
        z_ref[0] = z
    c = zb * y
    for g in range(CONV_GROUPS):
        sl = slice(g * LANES, (g + 1) * LANES)
        convn_ref[0, :, sl] = _bf(_rms(c[:, sl], gconv_ref[:, sl]))


def _proj(x, cs, s0, s1, wts, seq_conv):
    b, s, _ = x.shape
    tm = min(512, s)
    grid = (b, s // tm)
    z_rows = SUBLANES if seq_conv else tm
    full = lambda shape: pl.BlockSpec(shape, lambda i, j: (0,) * len(shape))
    tok = lambda width: pl.BlockSpec((1, tm, width), lambda i, j: (i, j, 0))
    prev = pl.BlockSpec((s0.shape[0] if seq_conv else tm, CONV_WIDTH),
                        lambda i, j: (0 if seq_conv else j, 0))
    out_shapes = (
        jax.ShapeDtypeStruct((b, N_HEADS, s, QK_PAD), jnp.bfloat16),
        jax.ShapeDtypeStruct((b, s, QK_PAD), jnp.bfloat16),
        jax.ShapeDtypeStruct((b, s, KV_RANK), jnp.float32),
        jax.ShapeDtypeStruct((b, s, QK_ROPE), jnp.float32),
        jax.ShapeDtypeStruct((b, s, CONV_WIDTH), jnp.bfloat16),
        jax.ShapeDtypeStruct((b, z_rows, CONV_WIDTH), jnp.float32),
    )
    out_specs = (
        pl.BlockSpec((1, N_HEADS, tm, QK_PAD), lambda i, j: (i, 0, j, 0)),
        tok(QK_PAD), tok(KV_RANK), tok(QK_ROPE), tok(CONV_WIDTH),
        pl.BlockSpec((1, z_rows, CONV_WIDTH), lambda i, j: (i, 0, 0)),
    )
    return pl.pallas_call(
        functools.partial(_proj_kernel, seq_conv, tm),
        grid=grid,
        in_specs=[
            tok(D_MODEL), full((1, D_MODEL)), full((D_MODEL, D_INX)), full((1, Q_RANK)),
            full((Q_RANK, 2 * N_HEADS * QK_NOPE)), full((N_HEADS, QK_NOPE, KV_RANK)),
            full((1, KV_RANK)), full((CONV_K, CONV_WIDTH)),
            pl.BlockSpec((tm, LANES), lambda i, j: (j, 0)), prev, prev,
            full((1, CONV_WIDTH)),
        ],
        out_specs=out_specs,
        out_shape=out_shapes,
        scratch_shapes=[pltpu.VMEM((tm + 2 * SUBLANES, CONV_WIDTH), jnp.float32)],
        compiler_params=_cparams("arbitrary", "arbitrary"),
        name="proj_seq" if seq_conv else "proj_tok",
    )(x, wts["g_mix"], wts["w_in"], wts["g_q"], wts["w_q"], wts["w_uk"], wts["g_kv"],
      wts["conv_w"], cs, s0, s1, wts["g_conv"])


def _attn_out(o, wuv_ref, gattn_ref, hd):
    a = jnp.dot(_bf(o), wuv_ref[hd], preferred_element_type=jnp.float32)
    return _bf(_rms(a, gattn_ref[:, hd * V_DIM:(hd + 1) * V_DIM]))


SOFTMAX_ROWS = 128


def _flash_kernel(tq, th, q_ref, k_ref, wuv_ref, gattn_ref, out_ref, m_ref, l_ref, acc_ref,
                  s0_ref, s1_ref, p0_ref, p1_ref, a0_ref, a1_ref):
    qi = pl.program_id(1)
    ki = pl.program_id(2)
    rows = N_HEADS * tq
    last_k = (qi * tq + tq - 1) // (2 * th)

    @pl.when(ki == 0)
    def _():
        m_ref[...] = jnp.full(m_ref.shape, NEG_INF, jnp.float32)
        l_ref[...] = jnp.zeros(l_ref.shape, jnp.float32)
        acc_ref[...] = jnp.zeros(acc_ref.shape, jnp.float32)

    def softmax_half(masked, half, s_ref, p_ref, a_ref):
        for c in range(rows // SOFTMAX_ROWS):
            rs = pl.ds(c * SOFTMAX_ROWS, SOFTMAX_ROWS)
            s = s_ref[rs, :]
            if masked:
                t0 = (c % (tq // SOFTMAX_ROWS)) * SOFTMAX_ROWS
                qpos = qi * tq + t0 + lax.broadcasted_iota(jnp.int32, (SOFTMAX_ROWS, th), 0)
                kpos = (2 * ki + half) * th + lax.broadcasted_iota(jnp.int32, (SOFTMAX_ROWS, th), 1)
                s = jnp.where(kpos <= qpos, s, NEG_INF)
            m_prev = m_ref[rs, :]
            m_new = jnp.maximum(m_prev, jnp.max(s, axis=-1, keepdims=True))
            alpha = jnp.exp(m_prev - m_new)
            p = jnp.exp(s - jnp.concatenate([m_new] * (th // LANES), axis=1))
            l_ref[rs, :] = alpha * l_ref[rs, :] + jnp.sum(p, axis=-1, keepdims=True)
            a_ref[rs, :] = alpha
            p_ref[rs, :] = _bf(p)
            m_ref[rs, :] = m_new

    def step(masked):
        q = q_ref[0].reshape(rows, QK_PAD)
        k0 = k_ref[0, 0:th, :]
        k1 = k_ref[0, th:2 * th, :]
        s0_ref[...] = lax.dot_general(q, k0, _NT, preferred_element_type=jnp.float32)
        s1_ref[...] = lax.dot_general(q, k1, _NT, preferred_element_type=jnp.float32)
        softmax_half(masked, 0, s0_ref, p0_ref, a0_ref)
        pv0 = jnp.dot(p0_ref[...], k0[:, 0:KV_RANK], preferred_element_type=jnp.float32)
        softmax_half(masked, 1, s1_ref, p1_ref, a1_ref)
        pv1 = jnp.dot(p1_ref[...], k1[:, 0:KV_RANK], preferred_element_type=jnp.float32)
        wide = lambda a: jnp.concatenate([a] * (KV_RANK // LANES), axis=1)
        acc_ref[...] = (acc_ref[...] * wide(a0_ref[...]) + pv0) * wide(a1_ref[...]) + pv1

    unmasked = (2 * ki + 2) * th <= qi * tq + 1
    pl.when(unmasked)(functools.partial(step, False))
    pl.when(jnp.logical_not(unmasked) & (ki <= last_k))(functools.partial(step, True))

    @pl.when(ki == last_k)
    def _():
        o = acc_ref[...] / l_ref[:, 0:1]
        for hd in range(N_HEADS):
            out_ref[0, :, hd * V_DIM:(hd + 1) * V_DIM] = _attn_out(
                o[hd * tq:(hd + 1) * tq], wuv_ref, gattn_ref, hd)


def _flash(qcat, kcat, wts):
    b, _, s, _ = qcat.shape
    tq, th = 512, 512
    tk = 2 * th
    rows = N_HEADS * tq
    return pl.pallas_call(
        functools.partial(_flash_kernel, tq, th),
        grid=(b, s // tq, s // tk),
        in_specs=[
            pl.BlockSpec((1, N_HEADS, tq, QK_PAD), lambda i, q, k: (i, 0, q, 0)),
            pl.BlockSpec((1, tk, QK_PAD),
                         lambda i, q, k: (i, jnp.minimum(k, (q * tq + tq - 1) // tk), 0)),
            pl.BlockSpec((N_HEADS, KV_RANK, V_DIM), lambda i, q, k: (0, 0, 0)),
            pl.BlockSpec((1, ATTN_WIDTH), lambda i, q, k: (0, 0)),
        ],
        out_specs=pl.BlockSpec((1, tq, ATTN_WIDTH), lambda i, q, k: (i, q, 0)),
        out_shape=jax.ShapeDtypeStruct((b, s, ATTN_WIDTH), jnp.bfloat16),
        scratch_shapes=[pltpu.VMEM((rows, LANES), jnp.float32),
                        pltpu.VMEM((rows, LANES), jnp.float32),
                        pltpu.VMEM((rows, KV_RANK), jnp.float32),
                        pltpu.VMEM((rows, th), jnp.float32), pltpu.VMEM((rows, th), jnp.float32),
                        pltpu.VMEM((rows, th), jnp.bfloat16), pltpu.VMEM((rows, th), jnp.bfloat16),
                        pltpu.VMEM((rows, LANES), jnp.float32), pltpu.VMEM((rows, LANES), jnp.float32)],
        compiler_params=_cparams("arbitrary", "arbitrary", "arbitrary"),
        name="flash",
    )(qcat, kcat, wts["w_uv"], wts["g_attn"])


PAGES_PER_STEP = 64
Q_ROWS = 16


def _decode_kernel(n_steps, pt_ref, q_ref, knew_ref, ckv_hbm, kr_hbm, out_ref,
                   ckv_buf, kr_buf, sem, m_ref, l_ref, acc_ref):
    b = pl.program_id(0)
    j = pl.program_id(1)
    total_steps = pl.num_programs(0) * n_steps
    step = b * n_steps + j
    slot = step % 2

    def page_copies(page_of, sl):
        copies = []
        for i in range(PAGES_PER_STEP):
            page = page_of(i)
            copies.append(pltpu.make_async_copy(ckv_hbm.at[0, page], ckv_buf.at[sl, i], sem.at[0, sl]))
            copies.append(pltpu.make_async_copy(kr_hbm.at[0, page], kr_buf.at[sl, i], sem.at[1, sl]))
        return copies

    @pl.when(step == 0)
    def _():
        for c in page_copies(lambda i: pt_ref[0, i], 0):
            c.start()

    nxt = step + 1

    @pl.when(nxt < total_steps)
    def _():
        nb_, nj_ = nxt // n_steps, nxt % n_steps
        for c in page_copies(lambda i: pt_ref[nb_, nj_ * PAGES_PER_STEP + i], 1 - slot):
            c.start()

    for c in page_copies(lambda i: 0, slot):
        c.wait()

    @pl.when(j == 0)
    def _():
        m_ref[...] = jnp.full(m_ref.shape, NEG_INF, jnp.float32)
        l_ref[...] = jnp.zeros(l_ref.shape, jnp.float32)
        acc_ref[...] = jnp.zeros(acc_ref.shape, jnp.float32)

    q = q_ref[0]
    qlat = q[:, 0:KV_RANK]
    qrope = q[:, KV_RANK:KV_RANK + QK_ROPE]
    vals = [_bf(ckv_buf[slot, i]) for i in range(PAGES_PER_STEP)]
    s = jnp.concatenate(
        [lax.dot_general(qlat, v, _NT, preferred_element_type=jnp.float32)
         + jnp.dot(qrope, _bf(kr_buf[slot, i]), preferred_element_type=jnp.float32)
         for i, v in enumerate(vals)], axis=-1)
    m_prev = m_ref[...]
    m_new = jnp.maximum(m_prev, jnp.max(s, axis=-1, keepdims=True))
    alpha = jnp.exp(m_prev - m_new)
    p = jnp.exp(s - m_new[:, 0:1])
    l_new = alpha * l_ref[...] + jnp.sum(p, axis=-1, keepdims=True)
    acc = acc_ref[...] * alpha[:, 0:1]
    for i, v in enumerate(vals):
        acc = acc + jnp.dot(_bf(p[:, i * PAGE_SIZE:(i + 1) * PAGE_SIZE]), v,
                            preferred_element_type=jnp.float32)
    m_ref[...] = m_new
    l_ref[...] = l_new
    acc_ref[...] = acc

    @pl.when(j == n_steps - 1)
    def _():
        kn = knew_ref[0].astype(jnp.float32)
        s_new = jnp.sum(q.astype(jnp.float32) * kn, axis=-1, keepdims=True)
        m_fin = jnp.maximum(m_new, s_new)
        a2 = jnp.exp(m_new - m_fin)
        p_new = jnp.exp(s_new - m_fin[:, 0:1])
        l_fin = a2 * l_new + p_new
        acc_fin = acc * a2[:, 0:1] + _bf(p_new).astype(jnp.float32) * kn[:, 0:KV_RANK]
        out_ref[0] = acc_fin / l_fin[:, 0:1]


def _decode(page_table, q, knew, cache_ckv, cache_krope_t):
    nb, n_pages = page_table.shape
    n_steps = n_pages // PAGES_PER_STEP

    grid_spec = pltpu.PrefetchScalarGridSpec(
        num_scalar_prefetch=1,
        grid=(nb, n_steps),
        in_specs=[pl.BlockSpec((1, Q_ROWS, QK_PAD), lambda b, j, pt: (b, 0, 0)),
                  pl.BlockSpec((1, 1, QK_PAD), lambda b, j, pt: (b, 0, 0)),
                  pl.BlockSpec(memory_space=pl.ANY), pl.BlockSpec(memory_space=pl.ANY)],
        out_specs=pl.BlockSpec((1, Q_ROWS, KV_RANK), lambda b, j, pt: (b, 0, 0)),
        scratch_shapes=[pltpu.VMEM((2, PAGES_PER_STEP, PAGE_SIZE, KV_RANK), jnp.float32),
                        pltpu.VMEM((2, PAGES_PER_STEP, QK_ROPE, PAGE_SIZE), jnp.float32),
                        pltpu.SemaphoreType.DMA((2, 2)),
                        pltpu.VMEM((Q_ROWS, LANES), jnp.float32),
                        pltpu.VMEM((Q_ROWS, LANES), jnp.float32),
                        pltpu.VMEM((Q_ROWS, KV_RANK), jnp.float32)],
    )
    return pl.pallas_call(
        functools.partial(_decode_kernel, n_steps),
        grid_spec=grid_spec,
        out_shape=jax.ShapeDtypeStruct((nb, Q_ROWS, KV_RANK), jnp.float32),
        compiler_params=_cparams("arbitrary", "arbitrary"),
        name="decode",
    )(page_table, q, knew, cache_ckv, cache_krope_t)


def _attn_out_kernel(o_ref, wuv_ref, gattn_ref, out_ref):
    for hd in range(N_HEADS):
        out_ref[:, hd * V_DIM:(hd + 1) * V_DIM] = _attn_out(o_ref[hd], wuv_ref, gattn_ref, hd)


def _attn_out_call(o, wts):
    t = o.shape[1]
    return pl.pallas_call(
        _attn_out_kernel,
        out_shape=jax.ShapeDtypeStruct((t, ATTN_WIDTH), jnp.bfloat16),
        name="attn_out",
    )(o, wts["w_uv"], wts["g_attn"])


def _mix_kernel(attn_ref, conv_ref, x_ref, wo_ref, gffn_ref, wqry_ref, keys_ref,
                xp_ref, xn_ref, sc_ref):
    y = (jnp.dot(attn_ref[...], wo_ref[0:ATTN_WIDTH, :], preferred_element_type=jnp.float32)
         + jnp.dot(conv_ref[...], wo_ref[ATTN_WIDTH:, :], preferred_element_type=jnp.float32))
    xp = x_ref[...] + y
    xp_ref[...] = xp
    xn = _rms(xp, gffn_ref[...])
    xn_ref[...] = xn
    q = jnp.dot(_bf(xn), wqry_ref[...], preferred_element_type=jnp.float32)
    half = PEER_DK // 2
    for g in range(GROUPS):
        sc_ref[g] = lax.dot_general(keys_ref[g % 2], _bf(q[:, g * half:(g + 1) * half]), _NT,
                                    preferred_element_type=jnp.float32)


def _mix(attn, conv, x, wts):
    t = x.shape[0]
    tm = min(256, t)
    full = lambda shape: pl.BlockSpec(shape, lambda i: (0,) * len(shape))
    tok = lambda width: pl.BlockSpec((tm, width), lambda i: (i, 0))
    return pl.pallas_call(
        _mix_kernel,
        grid=(t // tm,),
        in_specs=[tok(ATTN_WIDTH), tok(CONV_WIDTH), tok(D_MODEL), full((D_MODEL, D_MODEL)),
                  full((1, D_MODEL)), full((D_MODEL, PEER_HEADS * PEER_DK)),
                  full((2, PEER_KEYS, PEER_DK // 2))],
        out_specs=(tok(D_MODEL), tok(D_MODEL),
                   pl.BlockSpec((GROUPS, PEER_KEYS, tm), lambda i: (0, 0, i))),
        out_shape=(jax.ShapeDtypeStruct((t, D_MODEL), jnp.float32),
                   jax.ShapeDtypeStruct((t, D_MODEL), jnp.float32),
                   jax.ShapeDtypeStruct((GROUPS, PEER_KEYS, t), jnp.float32)),
        compiler_params=_cparams("arbitrary"),
        name="mix",
    )(attn, conv, x, wts["w_o"], wts["g_ffn"], wts["w_query"], wts["sub_keys"])


def _top_rounds(s, ids, payload, n):
    vals, idxs, pays = [], [], []
    sentinel = jnp.int32(2 ** 30)
    for r in range(n):
        m = jnp.max(s, axis=0, keepdims=True)
        win = jnp.min(jnp.where(s == m, ids, sentinel), axis=0, keepdims=True)
        vals.append(m)
        idxs.append(win)
        if payload is not None:
            pays.append(jnp.max(jnp.where(ids == win, payload, -1), axis=0, keepdims=True))
        if r < n - 1:
            s = jnp.where(ids == win, NEG_INF, s)
    cat = lambda xs: jnp.concatenate(xs, axis=0)
    return cat(vals), cat(idxs), (cat(pays) if payload is not None else None)


def _top_rounds_keys(s, n):
    nt = s.shape[0] // SUBLANES
    sub = lax.broadcasted_iota(jnp.int32, (SUBLANES, s.shape[1]), 0)
    tiles = [s[SUBLANES * j:SUBLANES * (j + 1)] for j in range(nt)]
    idt = [sub + SUBLANES * j for j in range(nt)]
    vals, idxs = [], []
    sentinel = jnp.int32(2 ** 30)
    for r in range(n):
        nodes = list(zip(tiles, idt))
        while len(nodes) > 1:
            nxt = []
            for (va, ia), (vb, ib) in zip(nodes[0::2], nodes[1::2]):
                c = vb > va
                nxt.append((jnp.where(c, vb, va), jnp.where(c, ib, ia)))
            nodes = nxt
        v, i = nodes[0]
        m = jnp.max(v, axis=0, keepdims=True)
        win = jnp.min(jnp.where(v == m, i, sentinel), axis=0, keepdims=True)
        vals.append(m)
        idxs.append(win)
        if r < n - 1:
            tiles = [jnp.where(idt[j] == win, NEG_INF, tiles[j]) for j in range(nt)]
    cat = lambda xs: jnp.concatenate(xs, axis=0)
    return cat(vals), cat(idxs), None


def _topk_kernel(tt, sc_ref, idx_ref, gate_ref):
    k = PEER_TOPK
    key_ids = lax.broadcasted_iota(jnp.int32, (PEER_KEYS, tt), 0)
    blocks = [(a, a + 1, 0, k // (a + 1)) for a in range(k // 2)] + [(k // 2, k, 0, 1)]
    n_cand = sum((a1 - a0) * (b1 - b0) for a0, a1, b0, b1 in blocks)
    n_pad = -n_cand % SUBLANES
    for hd in range(PEER_HEADS):
        v0, i0, _ = _top_rounds_keys(sc_ref[2 * hd], k)
        v1, i1, _ = _top_rounds_keys(sc_ref[2 * hd + 1], k)
        cand, cid, eid = [], [], []
        for (a0, a1, b0, b1) in blocks:
            na, nb = a1 - a0, b1 - b0
            if na == 1:
                va, ia, vb, ib = v0[a0:a1], i0[a0:a1], v1[b0:b1], i1[b0:b1]
                rank = a0 * k + b0 + lax.broadcasted_iota(jnp.int32, (nb, tt), 0)
            else:
                va, ia, vb, ib = v0[a0:a1], i0[a0:a1], v1[b0:b1], i1[b0:b1]
                rank = (a0 + lax.broadcasted_iota(jnp.int32, (na, tt), 0)) * k + b0
            cand.append(va + vb)
            eid.append(ia * PEER_KEYS + ib)
            cid.append(rank)
        if n_pad:
            cand.append(jnp.full((n_pad, tt), NEG_INF, jnp.float32))
            eid.append(jnp.zeros((n_pad, tt), jnp.int32))
            cid.append(jnp.full((n_pad, tt), k * k, jnp.int32))
        cand = jnp.concatenate(cand, axis=0)
        cid = jnp.concatenate(cid, axis=0)
        eid = jnp.concatenate(eid, axis=0)
        ts, _, te = _top_rounds(cand, cid, eid, k)
        e = jnp.exp(ts - ts[0:1])
        g = e / jnp.sum(e, axis=0, keepdims=True)
        idx_ref[hd * k:(hd + 1) * k, :] = te * PACKED_ROWS
        gate_ref[hd * k:(hd + 1) * k, :] = g


def _topk(scores):
    t = scores.shape[2]
    tt = LANES
    return pl.pallas_call(
        functools.partial(_topk_kernel, tt),
        grid=(t // tt,),
        in_specs=[pl.BlockSpec((GROUPS, PEER_KEYS, tt), lambda i: (0, 0, i))],
        out_specs=(pl.BlockSpec((N_SEL, tt), lambda i: (0, i)),
                   pl.BlockSpec((N_SEL, tt), lambda i: (0, i))),
        out_shape=(jax.ShapeDtypeStruct((N_SEL, t), jnp.int32),
                   jax.ShapeDtypeStruct((N_SEL, t), jnp.float32)),
        compiler_params=_cparams("arbitrary"),
        name="topk",
    )(scores)


TOK_BLOCK = 128
TOK_GROUP = 8
N_SLOTS = 4
PACKED_ROWS = ROWS_PER_EXPERT // 2
PACK_BLOCK = 8192


def _split_bf16(x):
    hi = _bf(x)
    lo = _bf(x - hi.astype(jnp.float32))
    return hi, lo


def _pack_kernel(x_ref, o_ref):
    o_ref[...] = pltpu.bitcast(_bf(x_ref[...]), jnp.int32)


def _pack_table(tab):
    rows = tab.size // LANES
    return pl.pallas_call(
        _pack_kernel,
        grid=(rows // PACK_BLOCK,),
        in_specs=[pl.BlockSpec((PACK_BLOCK, LANES), lambda i: (i, 0))],
        out_specs=pl.BlockSpec((PACK_BLOCK // 2, LANES), lambda i: (i, 0)),
        out_shape=jax.ShapeDtypeStruct((rows // 2, LANES), jnp.int32),
        compiler_params=_cparams("arbitrary"),
        name="pack_table",
    )(tab.reshape(rows, LANES))


def _gather_tile(tab_ref, off):
    w = tab_ref[pl.ds(pl.multiple_of(off, PACKED_ROWS), PACKED_ROWS), :]
    return pltpu.bitcast(w, jnp.bfloat16).astype(jnp.float32)


def _token_tile(xg, tl):
    return jnp.concatenate(
        [xg[tl:tl + 1, r * LANES:(r + 1) * LANES] for r in range(ROWS_PER_EXPERT)], axis=0)


def _staged_groups(idx_ref, slots, sem, process):
    n_slots = len(slots)
    n_groups = idx_ref.shape[0] // TOK_GROUP
    assert n_slots % 2 == 0 and n_groups % n_slots == 0

    def copy(g, s):
        rows = pl.ds(pl.multiple_of(g * TOK_GROUP, TOK_GROUP), TOK_GROUP)
        return pltpu.make_async_copy(idx_ref.at[rows, :], slots[s], sem.at[s])

    for s in range(n_slots - 1):
        copy(s, s).start()

    def ring(i, carry):
        for s in range(n_slots):
            g = n_slots * i + s
            copy(g, s).wait()
            ahead = n_slots - 1

            @pl.when(g + ahead < n_groups)
            def _():
                copy(g + ahead, (s + ahead) % n_slots).start()

            process(g, s % 2, slots[s])
        return carry

    lax.fori_loop(0, n_groups // n_slots, ring, 0)
    return n_groups


FOLD_BLOCK = SUBLANES
P_PITCH = ROWS_PER_EXPERT + 1


def _peer_u_kernel(idx_ref, x_ref, gate_ref, tab_ref, out_ref, *scratch):
    slots, sem = scratch[:N_SLOTS], scratch[N_SLOTS]
    f0_ref, f1_ref = scratch[N_SLOTS + 1:N_SLOTS + 3]
    p_refs = scratch[N_SLOTS + 3:]
    ones = jnp.ones((SUBLANES, LANES), jnp.bfloat16)
    sub = lax.broadcasted_iota(jnp.int32, (SUBLANES, LANES), 0)
    f_refs = (f0_ref, f1_ref)

    def lane_fold(f_ref, tl, rows):
        f_hi, f_lo = _split_bf16(f_ref[tl * N_SEL:(tl + 1) * N_SEL, :])
        a = (lax.dot_general(ones, f_hi, _NT, preferred_element_type=jnp.float32)
             + lax.dot_general(ones, f_lo, _NT, preferred_element_type=jnp.float32))
        return a if rows is None else jnp.where(sub == tl, a, rows)

    def store_rows(group, rows):
        out_ref[pl.ds(pl.multiple_of(group * TOK_GROUP, TOK_GROUP), TOK_GROUP), :] = rows

    def process(g, s, slot):
        xg = x_ref[pl.ds(pl.multiple_of(g * TOK_GROUP, TOK_GROUP), TOK_GROUP), :]
        rows = None
        for tl in range(TOK_GROUP + 1):
            if tl < TOK_GROUP:
                xt = _token_tile(xg, tl)
                rows = lane_fold(f_refs[1 - s], tl, rows)
            for kb in range(N_SEL // FOLD_BLOCK):
                if tl < TOK_GROUP:
                    for k in range(kb * FOLD_BLOCK, (kb + 1) * FOLD_BLOCK):
                        p_refs[tl][k * P_PITCH:k * P_PITCH + ROWS_PER_EXPERT, :] = (
                            _gather_tile(tab_ref, slot[tl, k]) * xt)
                if tl >= 1:
                    p_ref = p_refs[tl - 1]
                    first = kb * FOLD_BLOCK * P_PITCH
                    parts = [p_ref[pl.ds(first + r, FOLD_BLOCK, stride=P_PITCH), :]
                             for r in range(ROWS_PER_EXPERT)]
                    while len(parts) > 1:
                        parts = [a + b for a, b in zip(parts[0::2], parts[1::2])]
                    row = (tl - 1) * N_SEL + kb * FOLD_BLOCK
                    f_refs[s][row:row + FOLD_BLOCK, :] = parts[0]
        store_rows(jnp.maximum(g - 1, 0), rows)

    f1_ref[...] = jnp.zeros(f1_ref.shape, jnp.float32)
    n_groups = _staged_groups(idx_ref, slots, sem, process)
    rows = None
    for tl in range(TOK_GROUP):
        rows = lane_fold(f_refs[(n_groups - 1) % 2], tl, rows)
    store_rows(n_groups - 1, rows)
    a = out_ref[...]
    out_ref[...] = gate_ref[...] * (0.5 * a * (1.0 + lax.erf(a * (2.0 ** -0.5))))


def _stage_scratch():
    return [pltpu.SMEM((TOK_GROUP, N_SEL), jnp.int32)] * N_SLOTS + [pltpu.SemaphoreType.DMA((N_SLOTS,))]


def _peer_u(idx, xn, gates, tab):
    t = idx.shape[0]
    tb = min(TOK_BLOCK, t)
    sel = pl.BlockSpec((tb, N_SEL), lambda i: (i, 0))
    return pl.pallas_call(
        _peer_u_kernel,
        grid=(t // tb,),
        in_specs=[sel, pl.BlockSpec((tb, D_MODEL), lambda i: (i, 0)), sel,
                  pl.BlockSpec(memory_space=pltpu.VMEM)],
        out_specs=sel,
        out_shape=jax.ShapeDtypeStruct((t, N_SEL), jnp.float32),
        scratch_shapes=_stage_scratch()
        + [pltpu.VMEM((TOK_GROUP * N_SEL, LANES), jnp.float32)] * 2
        + [pltpu.VMEM((N_SEL * P_PITCH, LANES), jnp.float32)] * TOK_GROUP,
        compiler_params=_cparams("arbitrary"),
        name="peer_u",
    )(idx, xn, gates, tab)


def _peer_v_kernel(idx_ref, w_ref, tab_ref, out_ref, *scratch):
    slots, sem = scratch[:N_SLOTS], scratch[N_SLOTS]
    wb0_ref, wb1_ref, acc_ref = scratch[N_SLOTS + 1:]
    ones = jnp.ones((LANES, LANES), jnp.bfloat16)
    eye = (lax.broadcasted_iota(jnp.int32, (N_SEL, LANES), 0)
           == lax.broadcasted_iota(jnp.int32, (N_SEL, LANES), 1)).astype(jnp.float32)
    wb_refs = (wb0_ref, wb1_ref)
    last_group = idx_ref.shape[0] // TOK_GROUP - 1

    def group_weights(group):
        return w_ref[pl.ds(pl.multiple_of(group * TOK_GROUP, TOK_GROUP), TOK_GROUP), :]

    def broadcast_weights(wb_ref, wg, tl):
        d_hi, d_lo = _split_bf16(eye * wg[tl:tl + 1, :])
        wb_ref[tl * N_SEL:(tl + 1) * N_SEL, :] = (
            jnp.dot(d_hi, ones, preferred_element_type=jnp.float32)
            + jnp.dot(d_lo, ones, preferred_element_type=jnp.float32))

    def process(g, s, slot):
        wg_next = group_weights(jnp.minimum(g + 1, last_group))
        wb_ref = wb_refs[s]
        for tl in range(TOK_GROUP):
            broadcast_weights(wb_refs[1 - s], wg_next, tl)
            acc = acc_ref.at[tl * ROWS_PER_EXPERT:(tl + 1) * ROWS_PER_EXPERT, :]
            for kb in range(N_SEL // FOLD_BLOCK):
                part = None
                for k in range(kb * FOLD_BLOCK, (kb + 1) * FOLD_BLOCK):
                    row = tl * N_SEL + k
                    w = jnp.broadcast_to(wb_ref[row:row + 1, :], (ROWS_PER_EXPERT, LANES))
                    term = w * _gather_tile(tab_ref, slot[tl, k])
                    part = term if part is None else part + term
                acc[...] = part if kb == 0 else acc[...] + part
        tiles = [acc_ref[tl * ROWS_PER_EXPERT:(tl + 1) * ROWS_PER_EXPERT, :] for tl in range(TOK_GROUP)]
        base = pl.multiple_of(g * TOK_GROUP, TOK_GROUP)
        for r in range(ROWS_PER_EXPERT):
            out_ref[pl.ds(base, TOK_GROUP), r * LANES:(r + 1) * LANES] = jnp.concatenate(
                [tile[r:r + 1, :] for tile in tiles], axis=0)

    wg0 = group_weights(0)
    for tl in range(TOK_GROUP):
        broadcast_weights(wb0_ref, wg0, tl)
    _staged_groups(idx_ref, slots, sem, process)


def _peer_v(idx, w, tab):
    t = idx.shape[0]
    tb = min(TOK_BLOCK, t)
    sel = pl.BlockSpec((tb, N_SEL), lambda i: (i, 0))
    return pl.pallas_call(
        _peer_v_kernel,
        grid=(t // tb,),
        in_specs=[sel, sel, pl.BlockSpec(memory_space=pltpu.VMEM)],
        out_specs=pl.BlockSpec((tb, D_MODEL), lambda i: (i, 0)),
        out_shape=jax.ShapeDtypeStruct((t, D_MODEL), jnp.float32),
        scratch_shapes=_stage_scratch() + [pltpu.VMEM((TOK_GROUP * N_SEL, LANES), jnp.float32)] * 2
        + [pltpu.VMEM((TOK_GROUP * ROWS_PER_EXPERT, LANES), jnp.float32)],
        compiler_params=_cparams("arbitrary"),
        name="peer_v",
    )(idx, w, tab)


def _final_kernel(xp_ref, o_ref, g_ref, y_ref):
    y_ref[...] = _rms(xp_ref[...] + o_ref[...], g_ref[...])


def _final(xp, o, g):
    t = xp.shape[0]
    tm = min(512, t)
    tok = pl.BlockSpec((tm, D_MODEL), lambda i: (i, 0))
    return pl.pallas_call(
        _final_kernel,
        grid=(t // tm,),
        in_specs=[tok, tok, pl.BlockSpec((1, D_MODEL), lambda i: (0, 0))],
        out_specs=tok,
        out_shape=jax.ShapeDtypeStruct((t, D_MODEL), jnp.float32),
        compiler_params=_cparams("arbitrary"),
        name="final",
    )(xp, o, g)


def _rope_table(pos):
    inv = ROPE_THETA ** (-jnp.arange(0, QK_ROPE, 2, dtype=jnp.float32) / QK_ROPE)
    ang = pos.astype(jnp.float32)[:, None] * inv[None, :]
    c, s = jnp.cos(ang), jnp.sin(ang)
    return jnp.concatenate([c, c, -s, s], axis=-1)


def _swap_halves(w):
    half = w.shape[-1] // 2
    return jnp.concatenate([w[..., half:], w[..., :half]], axis=-1)


def _prepare_weights(g_mix_norm, w_in, g_q, w_uq, g_kv, w_uk, w_uv, conv_w, g_attn_out, g_conv_out,
                     w_o, g_ffn_norm, w_query, sub_keys, expert_u, expert_v):
    o1 = Q_RANK
    o2 = o1 + KV_RANK
    o3 = o2 + QK_ROPE
    w_kr = w_in[:, o2:o3]
    w_inx = jnp.concatenate([w_in[:, :o3], _swap_halves(w_kr), w_in[:, o3:]], axis=1)
    q_nope = w_uq[:, :, :QK_NOPE].reshape(Q_RANK, N_HEADS * QK_NOPE)
    q_rope = w_uq[:, :, QK_NOPE:]
    q_pair = jnp.concatenate([q_rope, _swap_halves(q_rope)], axis=-1).reshape(Q_RANK, N_HEADS * LANES)
    row = lambda g: g.reshape(1, -1)
    return {
        "g_mix": row(g_mix_norm), "w_in": _bf(w_inx), "g_q": row(g_q),
        "w_q": _bf(jnp.concatenate([q_nope, q_pair], axis=1)),
        "w_uk": _bf(jnp.transpose(w_uk, (1, 2, 0))), "g_kv": row(g_kv),
        "w_uv": _bf(jnp.transpose(w_uv, (1, 0, 2))), "conv_w": conv_w,
        "g_attn": row(g_attn_out), "g_conv": row(g_conv_out), "w_o": _bf(w_o),
        "g_ffn": row(g_ffn_norm), "w_query": _bf(w_query), "sub_keys": _bf(sub_keys),
        "tab_u": _pack_table(expert_u), "tab_v": _pack_table(expert_v),
    }


def _channel_mixer(attn, conv, x, wts, g_final):
    t = x.shape[0]
    xp, xn, scores = _mix(attn, conv, x, wts)
    idx_t, gate_t = _topk(scores)
    idx, gates = idx_t.T, gate_t.T
    w = _peer_u(idx, xn, gates, wts["tab_u"])
    o = _peer_v(idx, w, wts["tab_v"])
    return _final(xp, o, g_final.reshape(1, D_MODEL))


def kernel(x_prompt, x_sample, cache_ckv, cache_krope, state_conv, page_table, g_mix_norm, w_in, g_q, w_uq, g_kv, w_uk, w_uv, conv_w, g_attn_out, g_conv_out, w_o, g_ffn_norm, w_query, sub_keys, expert_u, expert_v, g_final):
    assert w_in.shape[0] == 1, "single-layer step"
    wts = _prepare_weights(g_mix_norm[0], w_in[0], g_q[0], w_uq[0], g_kv[0], w_uk[0], w_uv[0], conv_w[0],
                           g_attn_out[0], g_conv_out[0], w_o[0], g_ffn_norm[0], w_query[0], sub_keys[0],
                           expert_u[0], expert_v[0])
    b, s, _ = x_prompt.shape
    nb, n_pages = page_table.shape
    past = n_pages * PAGE_SIZE

    zero_prev = jnp.zeros((SUBLANES, CONV_WIDTH), jnp.float32)
    qcat, kcat, ckv_p, kr_p, conv_p, tail_p = _proj(
        x_prompt, _rope_table(jnp.arange(s)), zero_prev, zero_prev, wts, True)
    attn_p = _flash(qcat, kcat, wts)
    y_prompt = _channel_mixer(attn_p.reshape(b * s, ATTN_WIDTH), conv_p.reshape(b * s, CONV_WIDTH),
                              x_prompt.reshape(b * s, D_MODEL), wts, g_final).reshape(b, s, D_MODEL)
    new_conv_p = tail_p[:, SUBLANES - (CONV_K - 1):, :]

    xs = x_sample.reshape(1, nb, D_MODEL)
    cs_s = jnp.broadcast_to(_rope_table(jnp.full((1,), past)), (nb, LANES))
    s0, s1 = state_conv[0, :, 0, :], state_conv[0, :, 1, :]
    qcat_s, kcat_s, ckv_s, kr_s, conv_s, z_s = _proj(xs, cs_s, s0, s1, wts, False)
    q_dec = jnp.pad(jnp.transpose(qcat_s[0], (1, 0, 2)), ((0, 0), (0, Q_ROWS - N_HEADS), (0, 0)))
    o_lat = _decode(page_table, q_dec, kcat_s.reshape(nb, 1, QK_PAD), cache_ckv,
                    jnp.swapaxes(cache_krope, 2, 3))
    attn_s = _attn_out_call(jnp.transpose(o_lat[:, :N_HEADS], (1, 0, 2)), wts)
    y_sample = _channel_mixer(attn_s, conv_s[0], x_sample.reshape(nb, D_MODEL), wts, g_final)
    new_conv_s = jnp.stack([s1, z_s[0]], axis=1)

    return (y_prompt, y_sample.reshape(nb, 1, D_MODEL),
            ckv_p[None], kr_p[None], new_conv_p[None],
            ckv_s.reshape(1, nb, 1, KV_RANK), kr_s.reshape(1, nb, 1, QK_ROPE), new_conv_s[None])
```
